```python
import math
import jax, jax.numpy as jnp
from jax import lax
import numpy as np

D_MODEL = 1024
BATCH = 16
SEQ = 2048
DEPTH = 1
DEC_BATCH = 16
DEC_SEQ = 4096
PAST_LEN = 128

SSD_EXPAND = 2
D_INNER = SSD_EXPAND * D_MODEL
SSD_HEAD_DIM = 64
SSD_HEADS = D_INNER // SSD_HEAD_DIM
SSD_GROUPS = 8
SSD_HPG = SSD_HEADS // SSD_GROUPS
SSD_STATE = 128
CONV_WIDTH = 5
SSD_CHUNK = 128
CONV_CH = D_INNER + 2 * SSD_GROUPS * SSD_STATE

MLA_HEADS = 16
Q_LORA = 384
KV_LORA = 256
QK_NOPE = 64
QK_ROPE = 32
V_HEAD = 64
ROPE_THETA = 10000.0
Q_BLOCK = 128

D_FF = 4 * D_MODEL
N_ADA = 6
EPS = 1e-6

IN_SIZES = (D_INNER,
            CONV_CH,
            2 * SSD_HEADS,
            Q_LORA,
            KV_LORA + QK_ROPE,
            2 * D_MODEL)
D_IN_PROJ = sum(IN_SIZES)

kernel_name = "hybrid_ssd_mla_gated_encoder"

F32 = jnp.float32


def _split(t, sizes):
    idx = [int(i) for i in np.cumsum(sizes)[:-1]]
    return jnp.split(t, idx, axis=-1)


def _rms_norm(x, g):
    xf = x.astype(F32)
    y = xf * lax.rsqrt(jnp.mean(jnp.square(xf), axis=-1, keepdims=True) + EPS)
    return (y * g.astype(F32)).astype(x.dtype)


def _dwconv(u, w, b):
    out = lax.conv_general_dilated(
        u, w[:, None, :].astype(u.dtype), window_strides=(1,),
        padding=[(CONV_WIDTH // 2, CONV_WIDTH // 2)],
        dimension_numbers=('NWC', 'WIO', 'NWC'),
        feature_group_count=u.shape[-1])
    return out + b.astype(u.dtype)


def _ssd_scan(x, dt, a, b_in, c_in):
    bsz, s = x.shape[:2]
    nc = s // SSD_CHUNK

    def chunk(t):
        return t.astype(F32).reshape((bsz, nc, SSD_CHUNK) + t.shape[2:])

    xc, dtc, bc, cc = chunk(x), chunk(dt), chunk(b_in), chunk(c_in)
    cum = jnp.cumsum(dtc * a, axis=2)
    seg = cum[:, :, :, None] - cum[:, :, None, :]
    lower = jnp.tril(jnp.ones((SSD_CHUNK, SSD_CHUNK), bool))[None, None, :, :, None, None]
    decay = jnp.exp(jnp.where(lower, seg, -jnp.inf))
    cb = jnp.einsum('bclgn,bcsgn->bclsg', cc, bc)
    m = cb[..., None] * decay * dtc[:, :, None]
    y_diag = jnp.einsum('bclsge,bcsgep->bclgep', m, xc)
    decay_end = jnp.exp(cum[:, :, -1:] - cum)
    xw = xc * (decay_end * dtc)[..., None]
    states = jnp.einsum('bcsgn,bcsgep->bcgepn', bc, xw)
    chunk_decay = jnp.exp(cum[:, :, -1])

    def step(h, inp):
        st, dec = inp
        return h * dec[..., None, None] + st, h

    h0 = jnp.zeros((bsz, SSD_GROUPS, SSD_HPG, SSD_HEAD_DIM, SSD_STATE), F32)
    _, prev = lax.scan(step, h0, (jnp.moveaxis(states, 1, 0), jnp.moveaxis(chunk_decay, 1, 0)))
    y_off = jnp.einsum('bclgn,cbgepn->bclgep', cc, prev) * jnp.exp(cum)[..., None]
    return (y_diag + y_off).reshape(x.shape)


def _ssd_bidir(x, dt_f, dt_b, a_f, a_b, b_in, c_in):
    flip = lambda t: jnp.flip(t, axis=1)
    y_f = _ssd_scan(x, dt_f, a_f, b_in, c_in)
    y_b = flip(_ssd_scan(flip(x), flip(dt_b), a_b, flip(b_in), flip(c_in)))
    return y_f + y_b


def _rope_tables(s):
    inv = 1.0 / (ROPE_THETA ** (jnp.arange(0, QK_ROPE, 2, dtype=F32) / QK_ROPE))
    ang = jnp.arange(s, dtype=F32)[:, None] * inv[None, :]
    return jnp.cos(ang), jnp.sin(ang)


def _apply_rope(t, cos, sin):
    t1, t2 = jnp.split(t.astype(F32), 2, axis=-1)
    c, s_ = cos[:, None, :], sin[:, None, :]
    return jnp.concatenate([t1 * c - t2 * s_, t1 * s_ + t2 * c], axis=-1).astype(t.dtype)


def _mla(q_a, kv_a, g_qn, w_qb, g_kvn, w_kvb):
    bsz, s, _ = q_a.shape
    q = (_rms_norm(q_a, g_qn) @ w_qb).reshape(bsz, s, MLA_HEADS, QK_NOPE + QK_ROPE)
    q_nope, q_rope = q[..., :QK_NOPE], q[..., QK_NOPE:]
    c_kv, k_rope = kv_a[..., :KV_LORA], kv_a[..., KV_LORA:]
    kv = (_rms_norm(c_kv, g_kvn) @ w_kvb).reshape(bsz, s, MLA_HEADS, QK_NOPE + V_HEAD)
    k_nope, v = kv[..., :QK_NOPE], kv[..., QK_NOPE:]
    cos, sin = _rope_tables(s)
    q_rope = _apply_rope(q_rope, cos, sin)
    k_rope = _apply_rope(k_rope[:, :, None, :], cos, sin)[:, :, 0]
    scale = (QK_NOPE + QK_ROPE) ** -0.5
    nb = s // Q_BLOCK

    def blk(t):
        return jnp.moveaxis(t.reshape(bsz, nb, Q_BLOCK, MLA_HEADS, t.shape[-1]), 1, 0)

    def attend(qs):
        qn, qr = qs
        logits = (jnp.einsum('bqhd,bkhd->bhqk', qn, k_nope)
                  + jnp.einsum('bqhr,bkr->bhqk', qr, k_rope)).astype(F32) * scale
        p = jax.nn.softmax(logits, axis=-1).astype(v.dtype)
        return jnp.einsum('bhqk,bkhd->bqhd', p, v)

    o = lax.map(attend, (blk(q_nope), blk(q_rope)))
    return jnp.moveaxis(o, 0, 1).reshape(bsz, s, MLA_HEADS * V_HEAD)


def _layer(x, c, w_ada, b_ada, g_norm1, w_in, conv_w, conv_b, dt_bias_fwd, dt_bias_bwd,
           a_log_fwd, a_log_bwd, d_skip, g_ssd_norm, w_ssd_out, g_q_norm, w_q_b,
           g_kv_norm, w_kv_b, w_mla_out, w_o, g_norm2, w_mlp_in, w_mlp_out):
    bsz, s, _ = x.shape
    ada = jax.nn.silu(c) @ w_ada + b_ada
    shift1, scale1, gate1, shift2, scale2, gate2 = jnp.split(ada[:, None, :], N_ADA, axis=-1)

    h = _rms_norm(x, g_norm1) * (1.0 + scale1) + shift1
    z, xbc, dt_raw, q_a, kv_a, gates = _split(h @ w_in, IN_SIZES)

    xbc = jax.nn.silu(_dwconv(xbc, conv_w, conv_b))
    xs, b_in, c_in = _split(xbc, (D_INNER, SSD_GROUPS * SSD_STATE, SSD_GROUPS * SSD_STATE))
    xs = xs.reshape(bsz, s, SSD_GROUPS, SSD_HPG, SSD_HEAD_DIM)
    b_in = b_in.reshape(bsz, s, SSD_GROUPS, SSD_STATE)
    c_in = c_in.reshape(bsz, s, SSD_GROUPS, SSD_STATE)
    dt_raw = dt_raw.astype(F32)
    dt_f = jax.nn.softplus(dt_raw[..., :SSD_HEADS] + dt_bias_fwd.astype(F32)).reshape(bsz, s, SSD_GROUPS, SSD_HPG)
    dt_b = jax.nn.softplus(dt_raw[..., SSD_HEADS:] + dt_bias_bwd.astype(F32)).reshape(bsz, s, SSD_GROUPS, SSD_HPG)
    a_f = -jnp.exp(a_log_fwd.astype(F32)).reshape(SSD_GROUPS, SSD_HPG)
    a_b = -jnp.exp(a_log_bwd.astype(F32)).reshape(SSD_GROUPS, SSD_HPG)
    y = _ssd_bidir(xs, dt_f, dt_b, a_f, a_b, b_in, c_in)
    y = y + d_skip.astype(F32).reshape(SSD_GROUPS, SSD_HPG)[..., None] * xs.astype(F32)
    y = y.astype(x.dtype).reshape(bsz, s, D_INNER)
    yg = (y * jax.nn.silu(z)).reshape(bsz, s, SSD_GROUPS, D_INNER // SSD_GROUPS)
    yg = _rms_norm(yg, jnp.ones((), F32)).reshape(bsz, s, D_INNER) * g_ssd_norm.astype(x.dtype)
    y_a = yg @ w_ssd_out

    y_b = _mla(q_a, kv_a, g_q_norm, w_q_b, g_kv_norm, w_kv_b) @ w_mla_out

    g_a, g_b = jnp.split(jax.nn.sigmoid(gates), 2, axis=-1)
    mixed = (g_a * y_a + g_b * y_b) @ w_o
    x = x + gate1 * mixed

    h2 = _rms_norm(x, g_norm2) * (1.0 + scale2) + shift2
    x = x + gate2 * (jnp.square(jax.nn.relu(h2 @ w_mlp_in)) @ w_mlp_out)
    return x


def setup_inputs(seed: int = 0) -> dict:
    key = jax.random.key(seed)
    ks = iter(jax.random.split(key, 40))

    def nrm(shape, scale):
        return jax.random.normal(next(ks), shape, F32) * scale

    def gain(shape):
        return 1.0 + nrm(shape, 0.05)

    L = DEPTH
    dt0 = jnp.exp(jax.random.uniform(next(ks), (L, SSD_HEADS), F32,
                                     math.log(1e-3), math.log(1e-1)))
    dt1 = jnp.exp(jax.random.uniform(next(ks), (L, SSD_HEADS), F32,
                                     math.log(1e-3), math.log(1e-1)))
    inv_sp = lambda d: d + jnp.log(-jnp.expm1(-d))
    return {
        "x_prompt": nrm((BATCH, SEQ, D_MODEL), 1.0),
        "x_sample": nrm((DEC_BATCH, DEC_SEQ, D_MODEL), 1.0),
        "c_prompt": nrm((BATCH, D_MODEL), 1.0),
        "c_sample": nrm((DEC_BATCH, D_MODEL), 1.0),
        "w_ada": nrm((L, D_MODEL, N_ADA * D_MODEL), 0.5 * D_MODEL ** -0.5),
        "b_ada": nrm((L, N_ADA * D_MODEL), 0.02),
        "g_norm1": gain((L, D_MODEL)),
        "w_in": nrm((L, D_MODEL, D_IN_PROJ), D_MODEL ** -0.5),
        "conv_w": nrm((L, CONV_WIDTH, CONV_CH), CONV_WIDTH ** -0.5),
        "conv_b": nrm((L, CONV_CH), 0.02),
        "dt_bias_fwd": inv_sp(dt0),
        "dt_bias_bwd": inv_sp(dt1),
        "a_log_fwd": jnp.log(jax.random.uniform(next(ks), (L, SSD_HEADS), F32, 1.0, 16.0)),
        "a_log_bwd": jnp.log(jax.random.uniform(next(ks), (L, SSD_HEADS), F32, 1.0, 16.0)),
        "d_skip": 1.0 + nrm((L, SSD_HEADS), 0.1),
        "g_ssd_norm": gain((L, D_INNER)),
        "w_ssd_out": nrm((L, D_INNER, D_MODEL), D_INNER ** -0.5),
        "g_q_norm": gain((L, Q_LORA)),
        "w_q_b": nrm((L, Q_LORA, MLA_HEADS * (QK_NOPE + QK_ROPE)), Q_LORA ** -0.5),
        "g_kv_norm": gain((L, KV_LORA)),
        "w_kv_b": nrm((L, KV_LORA, MLA_HEADS * (QK_NOPE + V_HEAD)), KV_LORA ** -0.5),
        "w_mla_out": nrm((L, MLA_HEADS * V_HEAD, D_MODEL), (MLA_HEADS * V_HEAD) ** -0.5),
        "w_o": nrm((L, D_MODEL, D_MODEL), D_MODEL ** -0.5),
        "g_norm2": gain((L, D_MODEL)),
        "w_mlp_in": nrm((L, D_MODEL, D_FF), D_MODEL ** -0.5),
        "w_mlp_out": nrm((L, D_FF, D_MODEL), D_FF ** -0.5),
        "g_final": gain((D_MODEL,)),
    }


def reference(x_prompt, x_sample, c_prompt, c_sample, w_ada, b_ada, g_norm1, w_in, conv_w,
              conv_b, dt_bias_fwd, dt_bias_bwd, a_log_fwd, a_log_bwd, d_skip, g_ssd_norm,
              w_ssd_out, g_q_norm, w_q_b, g_kv_norm, w_kv_b, w_mla_out, w_o, g_norm2,
              w_mlp_in, w_mlp_out, g_final):
    def trunk(x, c):
        for l in range(DEPTH):
            x = _layer(x, c, w_ada[l], b_ada[l], g_norm1[l], w_in[l], conv_w[l], conv_b[l],
                       dt_bias_fwd[l], dt_bias_bwd[l], a_log_fwd[l], a_log_bwd[l], d_skip[l],
                       g_ssd_norm[l], w_ssd_out[l], g_q_norm[l], w_q_b[l], g_kv_norm[l],
                       w_kv_b[l], w_mla_out[l], w_o[l], g_norm2[l], w_mlp_in[l], w_mlp_out[l])
        return _rms_norm(x, g_final)

    y_prompt = trunk(x_prompt, c_prompt)
    y_sample = trunk(x_sample, c_sample)
    return (y_prompt, y_sample)
```

```python
import functools

import jax
import jax.numpy as jnp
import numpy as np
from jax import lax
from jax.experimental import pallas as pl
from jax.experimental.pallas import tpu as pltpu

F32 = jnp.float32
BF16 = jnp.bfloat16

SSD_HEAD_DIM = 64
SSD_GROUPS = 8
SSD_STATE = 128
SSD_CHUNK = 128
CONV_WIDTH = 5
MLA_HEADS = 16
Q_LORA = 384
KV_LORA = 256
QK_NOPE = 64
QK_ROPE = 32
V_HEAD = 64
ROPE_THETA = 10000.0
N_ADA = 6
EPS = 1e-6

LANES = 128
HEAD_PAD = 128
VMEM_LIMIT = 56 * 1024 * 1024

HPG = 4
GW = HPG * SSD_HEAD_DIM
HALO = 16

CKV_OFF = Q_LORA
KR_OFF = CKV_OFF + KV_LORA
KRS_OFF = KR_OFF + LANES
SMALL_W = KRS_OFF + LANES


def _cparams(n_axes):
    return pltpu.CompilerParams(
        dimension_semantics=("arbitrary",) * n_axes, vmem_limit_bytes=VMEM_LIMIT)


def _resident(shape):
    nd = len(shape)
    return pl.BlockSpec(shape, lambda *_: (0,) * nd, pipeline_mode=pl.Buffered(1))


def _rms(x):
    return x * lax.rsqrt(jnp.mean(x * x, axis=-1, keepdims=True) + EPS)


def _silu(x):
    return x * jax.nn.sigmoid(x)


def _ada_kernel(c_ref, w_ref, b_ref, o_ref):
    s = _silu(c_ref[...])
    o_ref[...] = jnp.dot(s, w_ref[...], precision=lax.Precision.HIGHEST,
                         preferred_element_type=F32) + b_ref[...]


def _ada(c, w, b):
    n, d = c.shape
    dout = w.shape[1]
    tn = d
    return pl.pallas_call(
        _ada_kernel,
        out_shape=jax.ShapeDtypeStruct((n, dout), F32),
        grid=(dout // tn,),
        in_specs=[pl.BlockSpec((n, d), lambda j: (0, 0)),
                  pl.BlockSpec((d, tn), lambda j: (0, j)),
                  pl.BlockSpec((1, tn), lambda j: (0, j))],
        out_specs=pl.BlockSpec((n, tn), lambda j: (0, j)),
        compiler_params=_cparams(1),
        name="ada",
    )(c, w, b.reshape(1, dout))


def _inproj_kernel(x_ref, sc_ref, sh_ref, g_ref, wm_ref, ws_ref, wdt_ref,
                   main_ref, small_ref, dtt_ref, *, col_chunk):
    h = _rms(x_ref[0]) * g_ref[...]
    h = h * (1.0 + sc_ref[0]) + sh_ref[0]
    hb = h.astype(BF16)
    for j in range(wm_ref.shape[1] // col_chunk):
        sl = slice(j * col_chunk, (j + 1) * col_chunk)
        main_ref[0, :, sl] = jnp.dot(hb, wm_ref[:, sl], preferred_element_type=F32).astype(BF16)
    small_ref[0] = jnp.dot(hb, ws_ref[...], preferred_element_type=F32)
    dtt_ref[0] = lax.dot_general(wdt_ref[...], hb, (((1,), (1,)), ((), ())),
                                 preferred_element_type=F32)


def _inproj(x, scale, shift, g, wm, ws, wdt, tm):
    b, s, d = x.shape
    nm, ns, nh = wm.shape[1], ws.shape[1], wdt.shape[0]
    return pl.pallas_call(
        functools.partial(_inproj_kernel, col_chunk=2048),
        out_shape=(jax.ShapeDtypeStruct((b, s, nm), BF16),
                   jax.ShapeDtypeStruct((b, s, ns), F32),
                   jax.ShapeDtypeStruct((b, nh, s), F32)),
        grid=(b, s // tm),
        in_specs=[pl.BlockSpec((1, tm, d), lambda i, j: (i, j, 0)),
                  pl.BlockSpec((1, 1, d), lambda i, j: (i, 0, 0)),
                  pl.BlockSpec((1, 1, d), lambda i, j: (i, 0, 0)),
                  _resident(g.shape), _resident(wm.shape), _resident(ws.shape),
                  _resident(wdt.shape)],
        out_specs=(pl.BlockSpec((1, tm, nm), lambda i, j: (i, j, 0)),
                   pl.BlockSpec((1, tm, ns), lambda i, j: (i, j, 0)),
                   pl.BlockSpec((1, nh, tm), lambda i, j: (i, 0, j))),
        compiler_params=_cparams(2),
        name="in_proj",
    )(x, scale, shift, g, wm, ws, wdt)


def _qkv_kernel(sm_ref, qc_ref, qs_ref, kc_ref, ks_ref, gq_ref, wq_ref, wqs_ref,
                gkv_ref, wk_ref, wv_ref, q_ref, k_ref, v_ref):
    sm = sm_ref[0]
    qn = (_rms(sm[:, 0:Q_LORA]) * gq_ref[...]).astype(BF16)
    cn = (_rms(sm[:, CKV_OFF:KR_OFF]) * gkv_ref[...]).astype(BF16)
    krope = sm[:, KR_OFF:KRS_OFF] * kc_ref[...] + sm[:, KRS_OFF:SMALL_W] * ks_ref[...]
    qcos, qsin = qc_ref[...], qs_ref[...]
    q1 = jnp.dot(qn, wq_ref[...], preferred_element_type=F32)
    q2 = jnp.dot(qn, wqs_ref[...], preferred_element_type=F32)
    k1 = jnp.dot(cn, wk_ref[...], preferred_element_type=F32)
    for h in range(MLA_HEADS):
        sl = slice(h * HEAD_PAD, (h + 1) * HEAD_PAD)
        q_ref[0, :, sl] = (q1[:, sl] * qcos + q2[:, sl] * qsin).astype(BF16)
        k_ref[0, :, sl] = (k1[:, sl] + krope).astype(BF16)
    v_ref[0] = jnp.dot(cn, wv_ref[...], preferred_element_type=F32).astype(BF16)


def _qkv(small, tabs, gq, wq, wqs, gkv, wk, wv, tm):
    b, s, ns = small.shape
    nq, nv = wq.shape[1], wv.shape[1]
    tab_spec = pl.BlockSpec((tm, LANES), lambda i, j: (j, 0))
    return pl.pallas_call(
        _qkv_kernel,
        out_shape=(jax.ShapeDtypeStruct((b, s, nq), BF16),
                   jax.ShapeDtypeStruct((b, s, nq), BF16),
                   jax.ShapeDtypeStruct((b, s, nv), BF16)),
        grid=(b, s // tm),
        in_specs=[pl.BlockSpec((1, tm, ns), lambda i, j: (i, j, 0)),
                  tab_spec, tab_spec, tab_spec, tab_spec,
                  _resident(gq.shape), _resident(wq.shape), _resident(wqs.shape),
                  _resident(gkv.shape), _resident(wk.shape), _resident(wv.shape)],
        out_specs=(pl.BlockSpec((1, tm, nq), lambda i, j: (i, j, 0)),
                   pl.BlockSpec((1, tm, nq), lambda i, j: (i, j, 0)),
                   pl.BlockSpec((1, tm, nv), lambda i, j: (i, j, 0))),
        compiler_params=_cparams(2),
        name="qkv",
    )(small, *tabs, gq, wq, wqs, gkv, wk, wv)


def _attn_kernel(q_ref, k_ref, v_ref, o_ref, *, tk, heads_per_step):
    s = k_ref.shape[1]
    tq = q_ref.shape[1]
    for hh in range(heads_per_step):
        q = q_ref[0, :, hh * HEAD_PAD:(hh + 1) * HEAD_PAD]

        def body(j, carry, hh=hh, q=q):
            m, l, acc = carry
            r0 = pl.multiple_of(j * tk, tk)
            k = k_ref[0, pl.ds(r0, tk), hh * HEAD_PAD:(hh + 1) * HEAD_PAD]
            v = v_ref[0, pl.ds(r0, tk), hh * V_HEAD:(hh + 1) * V_HEAD]
            sc = lax.dot_general(q, k, (((1,), (1,)), ((), ())), preferred_element_type=F32)
            m_new = jnp.maximum(m, jnp.max(sc, axis=-1, keepdims=True))
            p = jnp.exp(sc - m_new)
            alpha = jnp.exp(m - m_new)
            l = alpha * l + jnp.sum(p, axis=-1, keepdims=True)
            acc = alpha * acc + jnp.dot(p.astype(BF16), v, preferred_element_type=F32)
            return m_new, l, acc

        init = (jnp.full((tq, 1), -jnp.inf, F32), jnp.zeros((tq, 1), F32),
                jnp.zeros((tq, V_HEAD), F32))
        _, l, acc = lax.fori_loop(0, s // tk, body, init)
        o_ref[0, :, hh * V_HEAD:(hh + 1) * V_HEAD] = (acc / l).astype(BF16)


def _attn(q, k, v, tq, tk):
    b, s, _ = q.shape
    hps = 2
    return pl.pallas_call(
        functools.partial(_attn_kernel, tk=tk, heads_per_step=hps),
        out_shape=jax.ShapeDtypeStruct((b, s, MLA_HEADS * V_HEAD), BF16),
        grid=(b, MLA_HEADS // hps, s // tq),
        in_specs=[pl.BlockSpec((1, tq, hps * HEAD_PAD), lambda i, h, j: (i, j, h)),
                  pl.BlockSpec((1, s, hps * HEAD_PAD), lambda i, h, j: (i, 0, h)),
                  pl.BlockSpec((1, s, hps * V_HEAD), lambda i, h, j: (i, 0, h))],
        out_specs=pl.BlockSpec((1, tq, hps * V_HEAD), lambda i, h, j: (i, j, h)),
        compiler_params=_cparams(3),
        name="attn",
    )(q, k, v)


def _lane_expand(cols, lane_head):
    out = cols[HPG - 1]
    for j in range(HPG - 2, -1, -1):
        out = jnp.where(lane_head <= j, cols[j], out)
    return out


def _ssd_kernel(z_ref, xr_ref, br_ref, cr_ref, dtt_ref, cwx_ref, cwb_ref, cwc_ref,
                cbx_ref, cbb_ref, cbc_ref, dtb_ref, alog_ref, dsk_ref, gn_ref,
                o_ref, xs_ref, bs_ref, cs_ref, acc_ref, col_ref, row_ref, h_ref):
    s = xr_ref.shape[1]
    q = SSD_CHUNK
    nc = s // q
    nh2 = 2 * HPG

    a = -jnp.exp(alog_ref[0])
    row_ref[0:nh2, :] = jax.nn.softplus(dtt_ref[0] + dtb_ref[0][:, 0:1])
    lane = lax.broadcasted_iota(jnp.int32, (nh2, q), 1)
    sub = lax.broadcasted_iota(jnp.int32, (nh2, q), 0)
    zeros_pad = jnp.zeros((q - 2 * nh2, q), F32)

    def prep_body(c, carry):
        r0 = pl.multiple_of(c * q, q)
        dt = row_ref[0:nh2, pl.ds(r0, q)]
        pre = suf = dt * a
        k = 1
        while k < q:
            pre = pre + jnp.where(lane >= k, pltpu.roll(pre, k, axis=1), 0.0)
            suf = suf + jnp.where(lane < q - k, pltpu.roll(suf, q - k, axis=1), 0.0)
            k *= 2
        cum = jnp.where(sub < HPG, pre, suf)
        row_ref[nh2:2 * nh2, pl.ds(r0, q)] = cum
        stacked = jnp.concatenate([cum, dt, zeros_pad], axis=0)
        col_ref[pl.ds(r0, q), :] = stacked.T
        return carry

    lax.fori_loop(0, nc, prep_body, 0)

    wx, wb, wc = cwx_ref[...], cwb_ref[...], cwc_ref[...]
    bx, bb, bc = cbx_ref[...], cbb_ref[...], cbc_ref[...]
    dsk = dsk_ref[0]
    pad = CONV_WIDTH // 2

    def conv_piece(ref, r0, c, w, bias):
        main = ref[0, pl.ds(r0, q), :].astype(F32)
        p0 = pl.multiple_of(jnp.maximum(r0 - HALO, 0), HALO)
        n0 = pl.multiple_of(jnp.minimum(r0 + q, s - HALO), HALO)
        prev = ref[0, pl.ds(p0, HALO), :].astype(F32) * jnp.where(c > 0, 1.0, 0.0)
        nxt = ref[0, pl.ds(n0, HALO), :].astype(F32) * jnp.where(c < nc - 1, 1.0, 0.0)
        window = jnp.concatenate([prev, main, nxt], axis=0)
        out = bias
        for t in range(CONV_WIDTH):
            lo = HALO - pad + t
            out = out + window[lo:lo + q] * w[t:t + 1, :]
        return _silu(out)

    def conv_body(c, carry):
        r0 = pl.multiple_of(c * q, q)
        xc = conv_piece(xr_ref, r0, c, wx, bx)
        xs_ref[pl.ds(r0, q), :] = xc
        acc_ref[pl.ds(r0, q), :] = xc * dsk
        bs_ref[pl.ds(r0, q), :] = conv_piece(br_ref, r0, c, wb, bb).astype(BF16)
        cs_ref[pl.ds(r0, q), :] = conv_piece(cr_ref, r0, c, wc, bc).astype(BF16)
        return carry

    lax.fori_loop(0, nc, conv_body, 0)

    h_ref[...] = jnp.zeros_like(h_ref)
    row_i = lax.broadcasted_iota(jnp.int32, (q, q), 0)
    col_i = lax.broadcasted_iota(jnp.int32, (q, q), 1)
    lane_head = lax.broadcasted_iota(jnp.int32, (q, GW), 1) // SSD_HEAD_DIM
    masks = (row_i >= col_i, row_i <= col_i)

    def direction(c, d):
        r0 = pl.multiple_of(c * q, q)
        xc = xs_ref[pl.ds(r0, q), :]
        bmat = bs_ref[pl.ds(r0, q), :]
        cmat = cs_ref[pl.ds(r0, q), :]
        colblk = col_ref[pl.ds(r0, q), :]
        cb = lax.dot_general(cmat, bmat, (((1,), (1,)), ((), ())), preferred_element_type=F32)
        ccols = [colblk[:, d * HPG + j:d * HPG + j + 1] for j in range(HPG)]
        dcols = [colblk[:, nh2 + d * HPG + j:nh2 + d * HPG + j + 1] for j in range(HPG)]
        cum_w = _lane_expand(ccols, lane_head)
        xdt = xc * _lane_expand(dcols, lane_head)
        xdt_b = xdt.astype(BF16)
        ydiag = None
        for j in range(HPG):
            crow = row_ref[nh2 + d * HPG + j:nh2 + d * HPG + j + 1, pl.ds(r0, q)]
            dec = jnp.exp(jnp.where(masks[d], ccols[j] - crow, -jnp.inf))
            m = (cb * dec).astype(BF16)
            xj = jnp.where(lane_head == j, xdt_b, jnp.zeros_like(xdt_b))
            term = jnp.dot(m, xj, preferred_element_type=F32)
            ydiag = term if ydiag is None else ydiag + term
        hprev = h_ref[d]
        yoff = jnp.dot(cmat, hprev.astype(BF16), preferred_element_type=F32) * jnp.exp(cum_w)
        acc_ref[pl.ds(r0, q), :] += ydiag + yoff
        end_row = cum_w[q - 1:q, :] if d == 0 else cum_w[0:1, :]
        xw = (xdt * jnp.exp(end_row - cum_w)).astype(BF16)
        upd = lax.dot_general(bmat, xw, (((0,), (0,)), ((), ())), preferred_element_type=F32)
        h_ref[d] = hprev * jnp.exp(end_row) + upd

    def scan_body(i, carry):
        direction(i, 0)
        direction(nc - 1 - i, 1)
        return carry

    lax.fori_loop(0, nc, scan_body, 0)

    gn = gn_ref[...]

    def out_body(c, carry):
        r0 = pl.multiple_of(c * q, q)
        y = acc_ref[pl.ds(r0, q), :] * _silu(z_ref[0, pl.ds(r0, q), :].astype(F32))
        o_ref[0, pl.ds(r0, q), :] = (_rms(y) * gn).astype(BF16)
        return carry

    lax.fori_loop(0, nc, out_body, 0)


def _ssd(main, dtt, cw, cb, dtb, alog, dsk, gn):
    b, s, _ = main.shape
    g = SSD_GROUPS
    n = SSD_STATE
    x_blk0 = (g * GW) // GW
    b_blk0 = (2 * g * GW) // n
    c_blk0 = b_blk0 + g
    cw_b0 = (g * GW) // n
    return pl.pallas_call(
        _ssd_kernel,
        out_shape=jax.ShapeDtypeStruct((b, s, g * GW), BF16),
        grid=(b, g),
        in_specs=[pl.BlockSpec((1, s, GW), lambda i, j: (i, 0, j)),
                  pl.BlockSpec((1, s, GW), lambda i, j: (i, 0, x_blk0 + j)),
                  pl.BlockSpec((1, s, n), lambda i, j: (i, 0, b_blk0 + j)),
                  pl.BlockSpec((1, s, n), lambda i, j: (i, 0, c_blk0 + j)),
                  pl.BlockSpec((1, 2 * HPG, s), lambda i, j: (i, j, 0)),
                  pl.BlockSpec((CONV_WIDTH, GW), lambda i, j: (0, j)),
                  pl.BlockSpec((CONV_WIDTH, n), lambda i, j: (0, cw_b0 + j)),
                  pl.BlockSpec((CONV_WIDTH, n), lambda i, j: (0, cw_b0 + g + j)),
                  pl.BlockSpec((1, GW), lambda i, j: (0, j)),
                  pl.BlockSpec((1, n), lambda i, j: (0, cw_b0 + j)),
                  pl.BlockSpec((1, n), lambda i, j: (0, cw_b0 + g + j)),
                  pl.BlockSpec((1, 2 * HPG, LANES), lambda i, j: (j, 0, 0)),
                  pl.BlockSpec((1, 2 * HPG, LANES), lambda i, j: (j, 0, 0)),
                  pl.BlockSpec((1, 1, GW), lambda i, j: (j, 0, 0)),
                  pl.BlockSpec((1, GW), lambda i, j: (0, j))],
        out_specs=pl.BlockSpec((1, s, GW), lambda i, j: (i, 0, j)),
        scratch_shapes=[pltpu.VMEM((s, GW), F32), pltpu.VMEM((s, n), BF16),
                        pltpu.VMEM((s, n), BF16), pltpu.VMEM((s, GW), F32),
                        pltpu.VMEM((s, LANES), F32), pltpu.VMEM((4 * HPG, s), F32),
                        pltpu.VMEM((2, n, GW), F32)],
        compiler_params=_cparams(2),
        name="ssd",
    )(main, main, main, main, dtt, cw, cw, cw, cb, cb, cb, dtb, alog, dsk, gn)


def _merge_kernel(yg_ref, at_ref, gt_ref, x_ref, g1_ref, wa_ref, wb_ref, wo_ref, o_ref):
    d = x_ref.shape[2]
    ya = jnp.dot(yg_ref[0], wa_ref[...], preferred_element_type=F32)
    yb = jnp.dot(at_ref[0], wb_ref[...], preferred_element_type=F32)
    gates = jax.nn.sigmoid(gt_ref[0].astype(F32))
    mixed = (gates[:, 0:d] * ya + gates[:, d:2 * d] * yb).astype(BF16)
    o_ref[0] = x_ref[0] + g1_ref[0] * jnp.dot(mixed, wo_ref[...], preferred_element_type=F32)


def _merge(yg, attn, main, x, gate1, wa, wb, wo, tm):
    b, s, d = x.shape
    gate_blk = (main.shape[2] - 2 * d) // (2 * d)
    return pl.pallas_call(
        _merge_kernel,
        out_shape=jax.ShapeDtypeStruct((b, s, d), F32),
        grid=(b, s // tm),
        in_specs=[pl.BlockSpec((1, tm, yg.shape[2]), lambda i, j: (i, j, 0)),
                  pl.BlockSpec((1, tm, attn.shape[2]), lambda i, j: (i, j, 0)),
                  pl.BlockSpec((1, tm, 2 * d), lambda i, j: (i, j, gate_blk)),
                  pl.BlockSpec((1, tm, d), lambda i, j: (i, j, 0)),
                  pl.BlockSpec((1, 1, d), lambda i, j: (i, 0, 0)),
                  _resident(wa.shape), _resident(wb.shape), _resident(wo.shape)],
        out_specs=pl.BlockSpec((1, tm, d), lambda i, j: (i, j, 0)),
        compiler_params=_cparams(2),
        name="merge",
    )(yg, attn, main, x, gate1, wa, wb, wo)


def _mlp_kernel(x_ref, sc_ref, sh_ref, g2_ref, gn_ref, gf_ref, w1_ref, w2_ref, o_ref,
                *, ff_chunk):
    x = x_ref[0]
    h = _rms(x) * gn_ref[...]
    hb = (h * (1.0 + sc_ref[0]) + sh_ref[0]).astype(BF16)
    acc = None
    for j in range(w1_ref.shape[1] // ff_chunk):
        sl = slice(j * ff_chunk, (j + 1) * ff_chunk)
        u = jnp.maximum(jnp.dot(hb, w1_ref[:, sl], preferred_element_type=F32), 0.0)
        term = jnp.dot((u * u).astype(BF16), w2_ref[sl, :], preferred_element_type=F32)
        acc = term if acc is None else acc + term
    y = x + g2_ref[0] * acc
    o_ref[0] = _rms(y) * gf_ref[...]


def _mlp(x, scale, shift, gate2, gn, gf, w1, w2, tm):
    b, s, d = x.shape
    vec = pl.BlockSpec((1, 1, d), lambda i, j: (i, 0, 0))
    return pl.pallas_call(
        functools.partial(_mlp_kernel, ff_chunk=1024),
        out_shape=jax.ShapeDtypeStruct((b, s, d), F32),
        grid=(b, s // tm),
        in_specs=[pl.BlockSpec((1, tm, d), lambda i, j: (i, j, 0)), vec, vec, vec,
                  _resident(gn.shape), _resident(gf.shape),
                  _resident(w1.shape), _resident(w2.shape)],
        out_specs=pl.BlockSpec((1, tm, d), lambda i, j: (i, j, 0)),
        compiler_params=_cparams(2),
        name="mlp",
    )(x, scale, shift, gate2, gn, gf, w1, w2)


def _rope_tables(s):
    inv = 1.0 / (ROPE_THETA ** (jnp.arange(0, QK_ROPE, 2, dtype=F32) / QK_ROPE))
    ang = jnp.arange(s, dtype=F32)[:, None] * inv[None, :]
    cos, sin = jnp.cos(ang), jnp.sin(ang)
    zn = jnp.zeros((s, QK_NOPE), F32)
    zp = jnp.zeros((s, HEAD_PAD - QK_NOPE - QK_ROPE), F32)
    kcos = jnp.concatenate([zn, cos, cos, zp], axis=1)
    ksin = jnp.concatenate([zn, -sin, sin, zp], axis=1)
    scale = (QK_NOPE + QK_ROPE) ** -0.5
    qcos = jnp.concatenate([jnp.ones((s, QK_NOPE), F32), cos, cos, zp], axis=1) * scale
    qsin = ksin * scale
    return qcos, qsin, kcos, ksin


def _prep_weights(w_in, w_q_b, w_kv_b, d_inner):
    d = w_in.shape[0]
    conv_ch = d_inner + 2 * SSD_GROUPS * SSD_STATE
    n_heads = d_inner // SSD_HEAD_DIM
    o_z, o_x = 0, d_inner
    o_dt = o_x + conv_ch
    o_q = o_dt + 2 * n_heads
    o_kv = o_q + Q_LORA
    o_g = o_kv + KV_LORA + QK_ROPE
    half = QK_ROPE // 2
    w_main = jnp.concatenate([w_in[:, o_z:o_dt], w_in[:, o_g:]], axis=1).astype(BF16)
    kr = w_in[:, o_kv + KV_LORA:o_g]
    kr_sw = jnp.concatenate([kr[:, half:], kr[:, :half]], axis=1)
    zl = jnp.zeros((d, QK_NOPE), F32)
    zr = jnp.zeros((d, HEAD_PAD - QK_NOPE - QK_ROPE), F32)
    w_small = jnp.concatenate(
        [w_in[:, o_q:o_kv], w_in[:, o_kv:o_kv + KV_LORA], zl, kr, zr, zl, kr_sw, zr],
        axis=1).astype(BF16)
    w_dt = w_in[:, o_dt:o_q].T.reshape(2, SSD_GROUPS, HPG, d)
    w_dt = jnp.transpose(w_dt, (1, 0, 2, 3)).reshape(2 * n_heads, d).astype(BF16)

    lq = w_q_b.shape[0]
    wq3 = w_q_b.reshape(lq, MLA_HEADS, QK_NOPE + QK_ROPE)
    q_nope, q_r = wq3[..., :QK_NOPE], wq3[..., QK_NOPE:]
    q_rsw = jnp.concatenate([q_r[..., half:], q_r[..., :half]], axis=-1)
    zq = jnp.zeros((lq, MLA_HEADS, HEAD_PAD - QK_NOPE - QK_ROPE), F32)
    wq = jnp.concatenate([q_nope, q_r, zq], axis=-1).reshape(lq, -1).astype(BF16)
    wqs = jnp.concatenate([jnp.zeros_like(q_nope), q_rsw, zq], axis=-1).reshape(lq, -1).astype(BF16)
    lk = w_kv_b.shape[0]
    wkv3 = w_kv_b.reshape(lk, MLA_HEADS, QK_NOPE + V_HEAD)
    zk = jnp.zeros((lk, MLA_HEADS, HEAD_PAD - QK_NOPE), F32)
    wk = jnp.concatenate([wkv3[..., :QK_NOPE], zk], axis=-1).reshape(lk, -1).astype(BF16)
    wv = wkv3[..., QK_NOPE:].reshape(lk, -1).astype(BF16)
    return w_main, w_small, w_dt, wq, wqs, wk, wv


def _group_rows(v):
    f, bwd = v
    g = jnp.concatenate([f.reshape(SSD_GROUPS, HPG), bwd.reshape(SSD_GROUPS, HPG)], axis=1)
    return jnp.broadcast_to(g[:, :, None], (SSD_GROUPS, 2 * HPG, LANES)).astype(F32)


def kernel(x_prompt, x_sample, c_prompt, c_sample, w_ada, b_ada, g_norm1, w_in, conv_w,
           conv_b, dt_bias_fwd, dt_bias_bwd, a_log_fwd, a_log_bwd, d_skip, g_ssd_norm,
           w_ssd_out, g_q_norm, w_q_b, g_kv_norm, w_kv_b, w_mla_out, w_o, g_norm2,
           w_mlp_in, w_mlp_out, g_final):
    assert w_ada.shape[0] == 1, "single layer"
    d = x_prompt.shape[2]
    d_inner = w_ssd_out.shape[1]
    w_main, w_small, w_dt, wq, wqs, wk, wv = _prep_weights(w_in[0], w_q_b[0], w_kv_b[0], d_inner)
    wa, wb, wo = (w_ssd_out[0].astype(BF16), w_mla_out[0].astype(BF16), w_o[0].astype(BF16))
    w1, w2 = w_mlp_in[0].astype(BF16), w_mlp_out[0].astype(BF16)
    row = lambda v: v.reshape(1, -1).astype(F32)
    dtb = _group_rows((dt_bias_fwd[0], dt_bias_bwd[0]))
    alog = _group_rows((a_log_fwd[0], a_log_bwd[0]))
    dsk = jnp.repeat(d_skip[0].astype(F32), SSD_HEAD_DIM).reshape(SSD_GROUPS, 1, GW)

    nb = c_prompt.shape[0]
    ada = _ada(jnp.concatenate([c_prompt, c_sample], axis=0), w_ada[0], b_ada[0])

    def trunk(x, ada_rows):
        b, s, _ = x.shape
        mods = [ada_rows[:, None, i * d:(i + 1) * d] for i in range(N_ADA)]
        shift1, scale1, gate1, shift2, scale2, gate2 = mods
        tm = min(512, s)
        main, small, dtt = _inproj(x, scale1, shift1, row(g_norm1[0]), w_main, w_small, w_dt, tm)
        q, k, v = _qkv(small, _rope_tables(s), row(g_q_norm[0]), wq, wqs,
                       row(g_kv_norm[0]), wk, wv, tm)
        attn = _attn(q, k, v, min(256, s), min(512, s))
        yg = _ssd(main, dtt, conv_w[0].astype(F32), row(conv_b[0]), dtb, alog, dsk,
                  row(g_ssd_norm[0]))
        x1 = _merge(yg, attn, main, x, gate1, wa, wb, wo, tm)
        return _mlp(x1, scale2, shift2, gate2, row(g_norm2[0]), row(g_final), w1, w2, tm)

    return trunk(x_prompt, ada[:nb]), trunk(x_sample, ada[nb:])
```

```python
import functools
import math

import jax
import jax.numpy as jnp
from jax import lax
from jax.experimental import pallas as pl
from jax.experimental.pallas import tpu as pltpu

F32 = jnp.float32
BF16 = jnp.bfloat16

SSD_HEAD_DIM = 64
SSD_GROUPS = 8
SSD_STATE = 128
SSD_CHUNK = 128
CONV_WIDTH = 5
MLA_HEADS = 16
Q_LORA = 384
KV_LORA = 256
QK_NOPE = 64
QK_ROPE = 32
V_HEAD = 64
ROPE_THETA = 10000.0
N_ADA = 6
EPS = 1e-6
LOG2E = math.log2(math.e)

LANES = 128
HEAD_PAD = 128
VMEM_LIMIT = 56 * 1024 * 1024

HPG = 4
GW = HPG * SSD_HEAD_DIM
HALO = 16

CKV_OFF = Q_LORA
KR_OFF = CKV_OFF + KV_LORA
KRS_OFF = KR_OFF + LANES
SMALL_W = KRS_OFF + LANES


def _cparams(n_axes):
    return pltpu.CompilerParams(
        dimension_semantics=("arbitrary",) * n_axes, vmem_limit_bytes=VMEM_LIMIT)


def _resident(shape):
    nd = len(shape)
    return pl.BlockSpec(shape, lambda *_: (0,) * nd, pipeline_mode=pl.Buffered(1))


def _rms(x):
    return x * lax.rsqrt(jnp.mean(x * x, axis=-1, keepdims=True) + EPS)


def _silu(x):
    return x * jax.nn.sigmoid(x)


def _ada_kernel(c_ref, w_ref, b_ref, o_ref):
    s = _silu(c_ref[...])
    o_ref[...] = jnp.dot(s, w_ref[...], precision=lax.Precision.HIGHEST,
                         preferred_element_type=F32) + b_ref[...]


def _ada(c, w, b):
    n, d = c.shape
    dout = w.shape[1]
    tn = d
    return pl.pallas_call(
        _ada_kernel,
        out_shape=jax.ShapeDtypeStruct((n, dout), F32),
        grid=(dout // tn,),
        in_specs=[pl.BlockSpec((n, d), lambda j: (0, 0)),
                  pl.BlockSpec((d, tn), lambda j: (0, j)),
                  pl.BlockSpec((1, tn), lambda j: (0, j))],
        out_specs=pl.BlockSpec((n, tn), lambda j: (0, j)),
        compiler_params=_cparams(1),
        name="ada",
    )(c, w, b.reshape(1, dout))


def _inproj_kernel(x_ref, sc_ref, sh_ref, g_ref, wm_ref, ws_ref, wdt_ref,
                   main_ref, small_ref, dtt_ref, *, col_chunk):
    h = _rms(x_ref[0]) * g_ref[...]
    h = h * (1.0 + sc_ref[0]) + sh_ref[0]
    hb = h.astype(BF16)
    for j in range(wm_ref.shape[1] // col_chunk):
        sl = slice(j * col_chunk, (j + 1) * col_chunk)
        main_ref[0, :, sl] = jnp.dot(hb, wm_ref[:, sl], preferred_element_type=F32).astype(BF16)
    small_ref[0] = jnp.dot(hb, ws_ref[...], preferred_element_type=F32)
    dtt = lax.dot_general(wdt_ref[...], hb, (((1,), (1,)), ((), ())), preferred_element_type=F32)
    for c in range(dtt_ref.shape[1]):
        dtt_ref[0, c] = dtt[:, c * SSD_CHUNK:(c + 1) * SSD_CHUNK]


def _inproj(x, scale, shift, g, wm, ws, wdt, tm):
    b, s, d = x.shape
    nm, ns, nh = wm.shape[1], ws.shape[1], wdt.shape[0]
    return pl.pallas_call(
        functools.partial(_inproj_kernel, col_chunk=2048),
        out_shape=(jax.ShapeDtypeStruct((b, s, nm), BF16),
                   jax.ShapeDtypeStruct((b, s, ns), F32),
                   jax.ShapeDtypeStruct((b, s // SSD_CHUNK, nh, SSD_CHUNK), F32)),
        grid=(b, s // tm),
        in_specs=[pl.BlockSpec((1, tm, d), lambda i, j: (i, j, 0)),
                  pl.BlockSpec((1, 1, d), lambda i, j: (i, 0, 0)),
                  pl.BlockSpec((1, 1, d), lambda i, j: (i, 0, 0)),
                  _resident(g.shape), _resident(wm.shape), _resident(ws.shape),
                  _resident(wdt.shape)],
        out_specs=(pl.BlockSpec((1, tm, nm), lambda i, j: (i, j, 0)),
                   pl.BlockSpec((1, tm, ns), lambda i, j: (i, j, 0)),
                   pl.BlockSpec((1, tm // SSD_CHUNK, nh, SSD_CHUNK), lambda i, j: (i, j, 0, 0))),
        compiler_params=_cparams(2),
        name="in_proj",
    )(x, scale, shift, g, wm, ws, wdt)


def _qkv_kernel(sm_ref, qc_ref, qs_ref, kc_ref, ks_ref, gq_ref, wq_ref, wqs_ref,
                gkv_ref, wk_ref, wv_ref, vone_ref, q_ref, k_ref, v_ref):
    sm = sm_ref[0]
    qn = (_rms(sm[:, 0:Q_LORA]) * gq_ref[...]).astype(BF16)
    cn = (_rms(sm[:, CKV_OFF:KR_OFF]) * gkv_ref[...]).astype(BF16)
    krope = sm[:, KR_OFF:KRS_OFF] * kc_ref[...] + sm[:, KRS_OFF:SMALL_W] * ks_ref[...]
    qcos, qsin = qc_ref[...], qs_ref[...]
    q1 = jnp.dot(qn, wq_ref[...], preferred_element_type=F32)
    q2 = jnp.dot(qn, wqs_ref[...], preferred_element_type=F32)
    k1 = jnp.dot(cn, wk_ref[...], preferred_element_type=F32)
    v = jnp.dot(cn, wv_ref[...], preferred_element_type=F32) + vone_ref[...]
    for h in range(MLA_HEADS):
        sl = slice(h * HEAD_PAD, (h + 1) * HEAD_PAD)
        q_ref[0, h] = (q1[:, sl] * qcos + q2[:, sl] * qsin).astype(BF16)
        k_ref[0, h] = (k1[:, sl] + krope).astype(BF16)
        v_ref[0, h] = v[:, sl].astype(BF16)


def _qkv(small, tabs, gq, wq, wqs, gkv, wk, wv, vone, tm):
    b, s, ns = small.shape
    tab_spec = pl.BlockSpec((tm, LANES), lambda i, j: (j, 0))
    head_major = jax.ShapeDtypeStruct((b, MLA_HEADS, s, HEAD_PAD), BF16)
    out_spec = pl.BlockSpec((1, MLA_HEADS, tm, HEAD_PAD), lambda i, j: (i, 0, j, 0))
    return pl.pallas_call(
        _qkv_kernel,
        out_shape=(head_major, head_major, head_major),
        grid=(b, s // tm),
        in_specs=[pl.BlockSpec((1, tm, ns), lambda i, j: (i, j, 0)),
                  tab_spec, tab_spec, tab_spec, tab_spec,
                  _resident(gq.shape), _resident(wq.shape), _resident(wqs.shape),
                  _resident(gkv.shape), _resident(wk.shape), _resident(wv.shape),
                  _resident(vone.shape)],
        out_specs=(out_spec, out_spec, out_spec),
        compiler_params=_cparams(2),
        name="qkv",
    )(small, *tabs, gq, wq, wqs, gkv, wk, wv, vone)


ATTN_HEADS_PER_STEP = 8


def _attn_kernel(q_ref, k_ref, v_ref, o_ref, s_ref, m_ref, *, tk):
    s = k_ref.shape[2]
    tq = q_ref.shape[2]
    npair = q_ref.shape[1] // 2
    nk = s // tk
    lane = lax.broadcasted_iota(jnp.int32, (tq, LANES), 1)

    def qk_pass(pp, slot):
        for e in range(2):
            q = q_ref[0, 2 * pp + e]
            mpart = jnp.full((tq, LANES), -jnp.inf, F32)
            for j in range(nk):
                sc = lax.dot_general(q, k_ref[0, 2 * pp + e, j * tk:(j + 1) * tk, :],
                                     (((1,), (1,)), ((), ())), preferred_element_type=F32)
                s_ref[slot, e, :, j * tk:(j + 1) * tk] = sc
                for c in range(tk // LANES):
                    mpart = jnp.maximum(mpart, sc[:, c * LANES:(c + 1) * LANES])
            m_ref[slot, e] = jnp.broadcast_to(jnp.max(mpart, axis=-1, keepdims=True), (tq, LANES))

    def pv_pass(pp, slot):
        out = None
        for e in range(2):
            mb = m_ref[slot, e]
            acc = jnp.zeros((tq, LANES), F32)
            for j in range(nk):
                cols = [jnp.exp2(s_ref[slot, e, :, j * tk + c * LANES:j * tk + (c + 1) * LANES]
                                 - mb).astype(BF16) for c in range(tk // LANES)]
                acc = acc + jnp.dot(jnp.concatenate(cols, axis=1),
                                    v_ref[0, 2 * pp + e, j * tk:(j + 1) * tk, :],
                                    preferred_element_type=F32)
            ones_lane = V_HEAD if e == 0 else 0
            is_value = (lane < V_HEAD) if e == 0 else (lane >= V_HEAD)
            o = jnp.where(is_value, acc / acc[:, ones_lane:ones_lane + 1], 0.0)
            out = o if out is None else out + o
        o_ref[0, pp] = out.astype(BF16)

    qk_pass(0, 0)

    def body(pp, carry):
        slot = pp % 2
        pv_pass(pp, slot)
        qk_pass(pp + 1, 1 - slot)
        return carry

    lax.fori_loop(0, npair - 1, body, 0)
    pv_pass(npair - 1, (npair - 1) % 2)


def _attn(q, k, v, tq, tk):
    b, nh, s, _ = q.shape
    hg = ATTN_HEADS_PER_STEP
    kv_spec = pl.BlockSpec((1, hg, s, HEAD_PAD), lambda i, h, j: (i, h, 0, 0),
                           pipeline_mode=pl.Buffered(1))
    return pl.pallas_call(
        functools.partial(_attn_kernel, tk=tk),
        out_shape=jax.ShapeDtypeStruct((b, nh // 2, s, 2 * V_HEAD), BF16),
        grid=(b, nh // hg, s // tq),
        in_specs=[pl.BlockSpec((1, hg, tq, HEAD_PAD), lambda i, h, j: (i, h, j, 0)),
                  kv_spec, kv_spec],
        out_specs=pl.BlockSpec((1, hg // 2, tq, 2 * V_HEAD), lambda i, h, j: (i, h, j, 0)),
        scratch_shapes=[pltpu.VMEM((2, 2, tq, s), F32), pltpu.VMEM((2, 2, tq, LANES), F32)],
        compiler_params=_cparams(3),
        name="attn",
    )(q, k, v)


def _pair_expand(cols, low_half):
    return jnp.concatenate([jnp.where(low_half, cols[0], cols[1]),
                            jnp.where(low_half, cols[2], cols[3])], axis=1)


def _ssd_kernel(z_ref, xbc_ref, dtt_ref, cw_ref, cbias_ref, dtb_ref, alog_ref, dsk_ref, gn_ref,
                o_ref, xs_ref, bt_ref, cs_ref, cb_ref, acc_ref, col_ref, row_ref, h_ref):
    s = xbc_ref.shape[1]
    q = SSD_CHUNK
    n = SSD_STATE
    nc = s // q
    nh2 = 2 * HPG

    a2 = -jnp.exp(alog_ref[0]) * LOG2E
    dt = jax.nn.softplus(dtt_ref[0] + dtb_ref[0][None]).reshape(nc * nh2, q)
    pre = suf = (dt.reshape(nc, nh2, q) * a2[None]).reshape(nc * nh2, q)
    lane = lax.broadcasted_iota(jnp.int32, (nc * nh2, q), 1)
    is_fwd = (lax.broadcasted_iota(jnp.int32, (nc * nh2, q), 0) % nh2) < HPG
    k = 1
    while k < q:
        pre = pre + jnp.where(lane >= k, pltpu.roll(pre, k, axis=1), 0.0)
        suf = suf + jnp.where(lane < q - k, pltpu.roll(suf, q - k, axis=1), 0.0)
        k *= 2
    cum = jnp.where(is_fwd, pre, suf)
    end = jnp.where(is_fwd, jnp.broadcast_to(cum[:, q - 1:q], cum.shape),
                    jnp.broadcast_to(cum[:, 0:1], cum.shape))
    row_ref[:, 0:nh2, :] = (cum - jnp.log2(dt)).reshape(nc, nh2, q)
    row_ref[:, nh2:2 * nh2, :] = (dt * jnp.exp2(end - cum)).reshape(nc, nh2, q)
    row_ref[:, 2 * nh2:3 * nh2, :] = jnp.exp2(end).reshape(nc, nh2, q)
    cum3 = cum.reshape(nc, nh2, q)
    zeros_pad = jnp.zeros((q - nh2, q), F32)
    for c in range(nc):
        col_ref[c] = jnp.concatenate([cum3[c], zeros_pad], axis=0).T

    w = cw_ref[...]
    bias = cbias_ref[...]
    dsk = dsk_ref[0]
    pad = CONV_WIDTH // 2

    win = 2 * q
    sh_r = lax.broadcasted_iota(jnp.int32, (q, win), 0)
    sh_c = lax.broadcasted_iota(jnp.int32, (q, win), 1)
    taps = [t for t in range(CONV_WIDTH) if t != pad]
    shifts = jnp.concatenate([jnp.where(sh_c == sh_r + (HALO - pad + t), 1.0, 0.0).astype(BF16)
                              for t in taps], axis=0)

    zero_halo = jnp.zeros((HALO, GW + 2 * n), BF16)
    tail = jnp.zeros((win - q - 2 * HALO, GW + 2 * n), BF16)

    def conv_body(c, carry):
        r0 = pl.multiple_of(c * q, q)
        main = xbc_ref[0, pl.ds(r0, q), :]
        p0 = pl.multiple_of(jnp.maximum(r0 - HALO, 0), HALO)
        n0 = pl.multiple_of(jnp.minimum(r0 + q, s - HALO), HALO)
        prev = jnp.where(c > 0, xbc_ref[0, pl.ds(p0, HALO), :], zero_halo)
        nxt = jnp.where(c < nc - 1, xbc_ref[0, pl.ds(n0, HALO), :], zero_halo)
        window = jnp.concatenate([prev, main, nxt, tail], axis=0)
        halves = []
        for lo in (0, GW):
            sl = slice(lo, lo + GW)
            out = bias[:, sl] + main[:, sl].astype(F32) * w[pad:pad + 1, sl]
            shifted = jnp.dot(shifts, window[:, sl], preferred_element_type=F32)
            for i, t in enumerate(taps):
                out = out + shifted[i * q:(i + 1) * q] * w[t:t + 1, sl]
            halves.append(_silu(out))
        xc = halves[0]
        bconv = halves[1][:, 0:n]
        cmat = halves[1][:, n:2 * n].astype(BF16)
        xs_ref[c] = xc.astype(BF16)
        acc_ref[c] = xc * dsk
        cs_ref[c] = cmat
        bt_ref[c] = bconv.T
        cb_ref[c] = lax.dot_general(cmat, bconv.astype(BF16), (((1,), (1,)), ((), ())),
                                    preferred_element_type=F32)
        return carry

    lax.fori_loop(0, nc, conv_body, 0, unroll=2)

    h_ref[...] = jnp.zeros_like(h_ref)
    row_i = lax.broadcasted_iota(jnp.int32, (q, q), 0)
    col_i = lax.broadcasted_iota(jnp.int32, (q, q), 1)
    low_half = col_i < SSD_HEAD_DIM
    low_row = low_half[0:1, :]
    lane_head = lax.broadcasted_iota(jnp.int32, (q, GW), 1) // SSD_HEAD_DIM
    masks = (row_i >= col_i, row_i <= col_i)

    def direction(c, d):
        xb = xs_ref[c]
        colblk = col_ref[c]
        cb = cb_ref[c]
        bt = bt_ref[c]
        zero_b = jnp.zeros_like(xb)
        row = lambda kind, j: row_ref[c, kind * nh2 + d * HPG + j:kind * nh2 + d * HPG + j + 1, :]
        cum_b = [jnp.broadcast_to(colblk[:, d * HPG + j:d * HPG + j + 1], (q, LANES))
                 for j in range(HPG)]
        ydiag = upd = None
        for jp in range(HPG // 2):
            ms, bts, xsel = [], [], []
            for j in (2 * jp, 2 * jp + 1):
                dec = jnp.exp2(jnp.where(masks[d], cum_b[j] - row(0, j), -jnp.inf))
                ms.append((cb * dec).astype(BF16))
                bts.append((bt * row(1, j)).astype(BF16))
                xsel.append(jnp.where(lane_head == j, xb, zero_b))
            xpair = jnp.concatenate(xsel, axis=0)
            t1 = jnp.dot(jnp.concatenate(ms, axis=1), xpair, preferred_element_type=F32)
            t2 = jnp.dot(jnp.concatenate(bts, axis=1), xpair, preferred_element_type=F32)
            ydiag = t1 if ydiag is None else ydiag + t1
            upd = t2 if upd is None else upd + t2
        hprev = h_ref[d]
        yoff = jnp.dot(cs_ref[c], hprev.astype(BF16), preferred_element_type=F32)
        acc_ref[c] += ydiag + yoff * jnp.exp2(_pair_expand(cum_b, low_half))
        h_ref[d] = hprev * _pair_expand([row(2, j) for j in range(HPG)], low_row) + upd

    def scan_body(i, carry):
        direction(i, 0)
        direction(nc - 1 - i, 1)
        return carry

    lax.fori_loop(0, nc, scan_body, 0, unroll=2)

    gn = gn_ref[...]

    def out_body(c, carry):
        r0 = pl.multiple_of(c * q, q)
        y = acc_ref[c] * _silu(z_ref[0, pl.ds(r0, q), :].astype(F32))
        o_ref[0, pl.ds(r0, q), :] = (_rms(y) * gn).astype(BF16)
        return carry

    lax.fori_loop(0, nc, out_body, 0, unroll=2)


def _ssd(main, dtt, cw, cb, dtb, alog, dsk, gn):
    b, s, _ = main.shape
    g = SSD_GROUPS
    n = SSD_STATE
    q = SSD_CHUNK
    nc = s // q
    gc = GW + 2 * n
    xbc_blk0 = (g * GW) // gc
    return pl.pallas_call(
        _ssd_kernel,
        out_shape=jax.ShapeDtypeStruct((b, s, g * GW), BF16),
        grid=(b, g),
        in_specs=[pl.BlockSpec((1, s, GW), lambda i, j: (i, 0, j)),
                  pl.BlockSpec((1, s, gc), lambda i, j: (i, 0, xbc_blk0 + j)),
                  pl.BlockSpec((1, nc, 2 * HPG, q), lambda i, j: (i, 0, j, 0)),
                  pl.BlockSpec((CONV_WIDTH, gc), lambda i, j: (0, j)),
                  pl.BlockSpec((1, gc), lambda i, j: (0, j)),
                  pl.BlockSpec((1, 2 * HPG, LANES), lambda i, j: (j, 0, 0)),
                  pl.BlockSpec((1, 2 * HPG, LANES), lambda i, j: (j, 0, 0)),
                  pl.BlockSpec((1, 1, GW), lambda i, j: (j, 0, 0)),
                  pl.BlockSpec((1, GW), lambda i, j: (0, j))],
        out_specs=pl.BlockSpec((1, s, GW), lambda i, j: (i, 0, j)),
        scratch_shapes=[pltpu.VMEM((nc, q, GW), BF16),
                        pltpu.VMEM((nc, n, q), F32),
                        pltpu.VMEM((nc, q, n), BF16),
                        pltpu.VMEM((nc, q, q), F32),
                        pltpu.VMEM((nc, q, GW), F32),
                        pltpu.VMEM((nc, q, LANES), F32),
                        pltpu.VMEM((nc, 6 * HPG, q), F32),
                        pltpu.VMEM((2, n, GW), F32)],
        compiler_params=_cparams(2),
        name="ssd",
    )(main, main, dtt, cw, cb, dtb, alog, dsk, gn)


def _merge_kernel(yg_ref, at_ref, gt_ref, x_ref, g1_ref, wa_ref, wb_ref, wo_ref, o_ref):
    d = x_ref.shape[2]
    ya = jnp.dot(yg_ref[0], wa_ref[...], preferred_element_type=F32)
    attn = jnp.concatenate([at_ref[0, p] for p in range(at_ref.shape[1])], axis=1)
    yb = jnp.dot(attn, wb_ref[...], preferred_element_type=F32)
    gates = jax.nn.sigmoid(gt_ref[0].astype(F32))
    mixed = (gates[:, 0:d] * ya + gates[:, d:2 * d] * yb).astype(BF16)
    o_ref[0] = x_ref[0] + g1_ref[0] * jnp.dot(mixed, wo_ref[...], preferred_element_type=F32)


def _merge(yg, attn, main, x, gate1, wa, wb, wo, tm):
    b, s, d = x.shape
    gate_blk = (main.shape[2] - 2 * d) // (2 * d)
    return pl.pallas_call(
        _merge_kernel,
        out_shape=jax.ShapeDtypeStruct((b, s, d), F32),
        grid=(b, s // tm),
        in_specs=[pl.BlockSpec((1, tm, yg.shape[2]), lambda i, j: (i, j, 0)),
                  pl.BlockSpec((1, attn.shape[1], tm, attn.shape[3]), lambda i, j: (i, 0, j, 0)),
                  pl.BlockSpec((1, tm, 2 * d), lambda i, j: (i, j, gate_blk)),
                  pl.BlockSpec((1, tm, d), lambda i, j: (i, j, 0)),
                  pl.BlockSpec((1, 1, d), lambda i, j: (i, 0, 0)),
                  _resident(wa.shape), _resident(wb.shape), _resident(wo.shape)],
        out_specs=pl.BlockSpec((1, tm, d), lambda i, j: (i, j, 0)),
        compiler_params=_cparams(2),
        name="merge",
    )(yg, attn, main, x, gate1, wa, wb, wo)


def _mlp_kernel(x_ref, sc_ref, sh_ref, g2_ref, gn_ref, gf_ref, w1_ref, w2_ref, o_ref,
                *, ff_chunk):
    x = x_ref[0]
    h = _rms(x) * gn_ref[...]
    hb = (h * (1.0 + sc_ref[0]) + sh_ref[0]).astype(BF16)
    acc = None
    for j in range(w1_ref.shape[1] // ff_chunk):
        sl = slice(j * ff_chunk, (j + 1) * ff_chunk)
        u = jnp.maximum(jnp.dot(hb, w1_ref[:, sl], preferred_element_type=F32), 0.0)
        term = jnp.dot((u * u).astype(BF16), w2_ref[sl, :], preferred_element_type=F32)
        acc = term if acc is None else acc + term
    y = x + g2_ref[0] * acc
    o_ref[0] = _rms(y) * gf_ref[...]


def _mlp(x, scale, shift, gate2, gn, gf, w1, w2, tm):
    b, s, d = x.shape
    vec = pl.BlockSpec((1, 1, d), lambda i, j: (i, 0, 0))
    return pl.pallas_call(
        functools.partial(_mlp_kernel, ff_chunk=1024),
        out_shape=jax.ShapeDtypeStruct((b, s, d), F32),
        grid=(b, s // tm),
        in_specs=[pl.BlockSpec((1, tm, d), lambda i, j: (i, j, 0)), vec, vec, vec,
                  _resident(gn.shape), _resident(gf.shape),
                  _resident(w1.shape), _resident(w2.shape)],
        out_specs=pl.BlockSpec((1, tm, d), lambda i, j: (i, j, 0)),
        compiler_params=_cparams(2),
        name="mlp",
    )(x, scale, shift, gate2, gn, gf, w1, w2)


def _rope_tables(s):
    inv = 1.0 / (ROPE_THETA ** (jnp.arange(0, QK_ROPE, 2, dtype=F32) / QK_ROPE))
    ang = jnp.arange(s, dtype=F32)[:, None] * inv[None, :]
    cos, sin = jnp.cos(ang), jnp.sin(ang)
    zn = jnp.zeros((s, QK_NOPE), F32)
    zp = jnp.zeros((s, HEAD_PAD - QK_NOPE - QK_ROPE), F32)
    kcos = jnp.concatenate([zn, cos, cos, zp], axis=1)
    ksin = jnp.concatenate([zn, -sin, sin, zp], axis=1)
    scale = (QK_NOPE + QK_ROPE) ** -0.5 * LOG2E
    qcos = jnp.concatenate([jnp.ones((s, QK_NOPE), F32), cos, cos, zp], axis=1) * scale
    qsin = ksin * scale
    return qcos, qsin, kcos, ksin


def _group_xbc(a, d_inner):
    lead = a.shape[:-1]
    nbc = SSD_GROUPS * SSD_STATE
    x = a[..., :d_inner].reshape(lead + (SSD_GROUPS, GW))
    bm = a[..., d_inner:d_inner + nbc].reshape(lead + (SSD_GROUPS, SSD_STATE))
    cm = a[..., d_inner + nbc:].reshape(lead + (SSD_GROUPS, SSD_STATE))
    return jnp.concatenate([x, bm, cm], axis=-1).reshape(lead + (-1,))


def _prep_weights(w_in, w_q_b, w_kv_b, d_inner):
    d = w_in.shape[0]
    conv_ch = d_inner + 2 * SSD_GROUPS * SSD_STATE
    n_heads = d_inner // SSD_HEAD_DIM
    o_z, o_x = 0, d_inner
    o_dt = o_x + conv_ch
    o_q = o_dt + 2 * n_heads
    o_kv = o_q + Q_LORA
    o_g = o_kv + KV_LORA + QK_ROPE
    half = QK_ROPE // 2
    w_main = jnp.concatenate([w_in[:, o_z:o_x], _group_xbc(w_in[:, o_x:o_dt], d_inner),
                              w_in[:, o_g:]], axis=1).astype(BF16)
    kr = w_in[:, o_kv + KV_LORA:o_g]
    kr_sw = jnp.concatenate([kr[:, half:], kr[:, :half]], axis=1)
    zl = jnp.zeros((d, QK_NOPE), F32)
    zr = jnp.zeros((d, HEAD_PAD - QK_NOPE - QK_ROPE), F32)
    w_small = jnp.concatenate(
        [w_in[:, o_q:o_kv], w_in[:, o_kv:o_kv + KV_LORA], zl, kr, zr, zl, kr_sw, zr],
        axis=1).astype(BF16)
    w_dt = w_in[:, o_dt:o_q].T.reshape(2, SSD_GROUPS, HPG, d)
    w_dt = jnp.transpose(w_dt, (1, 0, 2, 3)).reshape(2 * n_heads, d).astype(BF16)

    lq = w_q_b.shape[0]
    wq3 = w_q_b.reshape(lq, MLA_HEADS, QK_NOPE + QK_ROPE)
    q_nope, q_r = wq3[..., :QK_NOPE], wq3[..., QK_NOPE:]
    q_rsw = jnp.concatenate([q_r[..., half:], q_r[..., :half]], axis=-1)
    zq = jnp.zeros((lq, MLA_HEADS, HEAD_PAD - QK_NOPE - QK_ROPE), F32)
    wq = jnp.concatenate([q_nope, q_r, zq], axis=-1).reshape(lq, -1).astype(BF16)
    wqs = jnp.concatenate([jnp.zeros_like(q_nope), q_rsw, zq], axis=-1).reshape(lq, -1).astype(BF16)
    lk = w_kv_b.shape[0]
    wkv3 = w_kv_b.reshape(lk, MLA_HEADS, QK_NOPE + V_HEAD)
    zk = jnp.zeros((lk, MLA_HEADS, HEAD_PAD - QK_NOPE), F32)
    wk = jnp.concatenate([wkv3[..., :QK_NOPE], zk], axis=-1).reshape(lk, -1).astype(BF16)
    v4 = wkv3[..., QK_NOPE:].reshape(lk, MLA_HEADS // 2, 2, V_HEAD)
    zv = jnp.zeros((lk, MLA_HEADS // 2, V_HEAD), F32)
    wv = jnp.stack([jnp.concatenate([v4[:, :, 0], zv], axis=-1),
                    jnp.concatenate([zv, v4[:, :, 1]], axis=-1)], axis=2)
    wv = wv.reshape(lk, MLA_HEADS * HEAD_PAD).astype(BF16)
    pair_ones = jnp.zeros((2 * HEAD_PAD,), F32).at[V_HEAD].set(1.0).at[HEAD_PAD].set(1.0)
    vone = jnp.tile(pair_ones, MLA_HEADS // 2).reshape(1, MLA_HEADS * HEAD_PAD)
    return w_main, w_small, w_dt, wq, wqs, wk, wv, vone


def _group_rows(v):
    f, bwd = v
    g = jnp.concatenate([f.reshape(SSD_GROUPS, HPG), bwd.reshape(SSD_GROUPS, HPG)], axis=1)
    return jnp.broadcast_to(g[:, :, None], (SSD_GROUPS, 2 * HPG, LANES)).astype(F32)


def kernel(x_prompt, x_sample, c_prompt, c_sample, w_ada, b_ada, g_norm1, w_in, conv_w,
           conv_b, dt_bias_fwd, dt_bias_bwd, a_log_fwd, a_log_bwd, d_skip, g_ssd_norm,
           w_ssd_out, g_q_norm, w_q_b, g_kv_norm, w_kv_b, w_mla_out, w_o, g_norm2,
           w_mlp_in, w_mlp_out, g_final):
    assert w_ada.shape[0] == 1, "single layer"
    d = x_prompt.shape[2]
    d_inner = w_ssd_out.shape[1]
    w_main, w_small, w_dt, wq, wqs, wk, wv, vone = _prep_weights(
        w_in[0], w_q_b[0], w_kv_b[0], d_inner)
    wa, wb, wo = (w_ssd_out[0].astype(BF16), w_mla_out[0].astype(BF16), w_o[0].astype(BF16))
    w1, w2 = w_mlp_in[0].astype(BF16), w_mlp_out[0].astype(BF16)
    row = lambda v: v.reshape(1, -1).astype(F32)
    dtb = _group_rows((dt_bias_fwd[0], dt_bias_bwd[0]))
    alog = _group_rows((a_log_fwd[0], a_log_bwd[0]))
    dsk = jnp.repeat(d_skip[0].astype(F32), SSD_HEAD_DIM).reshape(SSD_GROUPS, 1, GW)
    cw = _group_xbc(conv_w[0].astype(F32), d_inner)
    cbias = _group_xbc(row(conv_b[0]), d_inner)

    nb = c_prompt.shape[0]
    ada = _ada(jnp.concatenate([c_prompt, c_sample], axis=0), w_ada[0], b_ada[0])

    def trunk(x, ada_rows):
        b, s, _ = x.shape
        mods = [ada_rows[:, None, i * d:(i + 1) * d] for i in range(N_ADA)]
        shift1, scale1, gate1, shift2, scale2, gate2 = mods
        tm = min(512, s)
        main, small, dtt = _inproj(x, scale1, shift1, row(g_norm1[0]), w_main, w_small, w_dt, tm)
        q, k, v = _qkv(small, _rope_tables(s), row(g_q_norm[0]), wq, wqs,
                       row(g_kv_norm[0]), wk, wv, vone, tm)
        attn = _attn(q, k, v, min(256, s), min(512, s))
        yg = _ssd(main, dtt, cw, cbias, dtb, alog, dsk, row(g_ssd_norm[0]))
        x1 = _merge(yg, attn, main, x, gate1, wa, wb, wo, tm)
        return _mlp(x1, scale2, shift2, gate2, row(g_norm2[0]), row(g_final), w1, w2, tm)

    return trunk(x_prompt, ada[:nb]), trunk(x_sample, ada[nb:])
```

```python
import functools
import math

import jax
import jax.numpy as jnp
from jax import lax
from jax.experimental import pallas as pl
from jax.experimental.pallas import tpu as pltpu

F32 = jnp.float32
BF16 = jnp.bfloat16

SSD_HEAD_DIM = 64
SSD_GROUPS = 8
SSD_STATE = 128
SSD_CHUNK = 128
CONV_WIDTH = 5
MLA_HEADS = 16
Q_LORA = 384
KV_LORA = 256
QK_NOPE = 64
QK_ROPE = 32
V_HEAD = 64
ROPE_THETA = 10000.0
N_ADA = 6
EPS = 1e-6
LOG2E = math.log2(math.e)

LANES = 128
HEAD_PAD = 128
VMEM_LIMIT = 56 * 1024 * 1024

HPG = 4
GW = HPG * SSD_HEAD_DIM
HALO = 16

CKV_OFF = Q_LORA
KR_OFF = CKV_OFF + KV_LORA
KRS_OFF = KR_OFF + LANES
SMALL_W = KRS_OFF + LANES


def _cparams(n_axes):
    return pltpu.CompilerParams(
        dimension_semantics=("arbitrary",) * n_axes, vmem_limit_bytes=VMEM_LIMIT)


def _resident(shape):
    nd = len(shape)
    return pl.BlockSpec(shape, lambda *_: (0,) * nd, pipeline_mode=pl.Buffered(1))


def _rms(x):
    return x * lax.rsqrt(jnp.mean(x * x, axis=-1, keepdims=True) + EPS)


def _silu(x):
    return x * jax.nn.sigmoid(x)


def _ada_kernel(c_ref, w_ref, b_ref, o_ref):
    s = _silu(c_ref[...])
    o_ref[...] = jnp.dot(s, w_ref[...], precision=lax.Precision.HIGHEST,
                         preferred_element_type=F32) + b_ref[...]


def _ada(c, w, b):
    n, d = c.shape
    dout = w.shape[1]
    tn = d
    return pl.pallas_call(
        _ada_kernel,
        out_shape=jax.ShapeDtypeStruct((n, dout), F32),
        grid=(dout // tn,),
        in_specs=[pl.BlockSpec((n, d), lambda j: (0, 0)),
                  pl.BlockSpec((d, tn), lambda j: (0, j)),
                  pl.BlockSpec((1, tn), lambda j: (0, j))],
        out_specs=pl.BlockSpec((n, tn), lambda j: (0, j)),
        compiler_params=_cparams(1),
        name="ada",
    )(c, w, b.reshape(1, dout))


def _inproj_kernel(x_ref, sc_ref, sh_ref, g_ref, wm_ref, ws_ref, wdt_ref,
                   main_ref, small_ref, dtt_ref, *, col_chunk):
    h = _rms(x_ref[0]) * g_ref[...]
    h = h * (1.0 + sc_ref[0]) + sh_ref[0]
    hb = h.astype(BF16)
    for j in range(wm_ref.shape[1] // col_chunk):
        sl = slice(j * col_chunk, (j + 1) * col_chunk)
        main_ref[0, :, sl] = jnp.dot(hb, wm_ref[:, sl], preferred_element_type=F32).astype(BF16)
    small_ref[0] = jnp.dot(hb, ws_ref[...], preferred_element_type=F32)
    dtt = lax.dot_general(wdt_ref[...], hb, (((1,), (1,)), ((), ())), preferred_element_type=F32)
    for c in range(dtt_ref.shape[1]):
        dtt_ref[0, c] = dtt[:, c * SSD_CHUNK:(c + 1) * SSD_CHUNK]


def _inproj(x, scale, shift, g, wm, ws, wdt, tm):
    b, s, d = x.shape
    nm, ns, nh = wm.shape[1], ws.shape[1], wdt.shape[0]
    return pl.pallas_call(
        functools.partial(_inproj_kernel, col_chunk=2048),
        out_shape=(jax.ShapeDtypeStruct((b, s, nm), BF16),
                   jax.ShapeDtypeStruct((b, s, ns), F32),
                   jax.ShapeDtypeStruct((b, s // SSD_CHUNK, nh, SSD_CHUNK), F32)),
        grid=(b, s // tm),
        in_specs=[pl.BlockSpec((1, tm, d), lambda i, j: (i, j, 0)),
                  pl.BlockSpec((1, 1, d), lambda i, j: (i, 0, 0)),
                  pl.BlockSpec((1, 1, d), lambda i, j: (i, 0, 0)),
                  _resident(g.shape), _resident(wm.shape), _resident(ws.shape),
                  _resident(wdt.shape)],
        out_specs=(pl.BlockSpec((1, tm, nm), lambda i, j: (i, j, 0)),
                   pl.BlockSpec((1, tm, ns), lambda i, j: (i, j, 0)),
                   pl.BlockSpec((1, tm // SSD_CHUNK, nh, SSD_CHUNK), lambda i, j: (i, j, 0, 0))),
        compiler_params=_cparams(2),
        name="in_proj",
    )(x, scale, shift, g, wm, ws, wdt)


def _qkv_kernel(sm_ref, qc_ref, qs_ref, kc_ref, ks_ref, gq_ref, wq_ref, wqs_ref,
                gkv_ref, wk_ref, wv_ref, vone_ref, q_ref, k_ref, v_ref):
    sm = sm_ref[0]
    qn = (_rms(sm[:, 0:Q_LORA]) * gq_ref[...]).astype(BF16)
    cn = (_rms(sm[:, CKV_OFF:KR_OFF]) * gkv_ref[...]).astype(BF16)
    krope = sm[:, KR_OFF:KRS_OFF] * kc_ref[...] + sm[:, KRS_OFF:SMALL_W] * ks_ref[...]
    qcos, qsin = qc_ref[...], qs_ref[...]
    q1 = jnp.dot(qn, wq_ref[...], preferred_element_type=F32)
    q2 = jnp.dot(qn, wqs_ref[...], preferred_element_type=F32)
    k1 = jnp.dot(cn, wk_ref[...], preferred_element_type=F32)
    v = jnp.dot(cn, wv_ref[...], preferred_element_type=F32) + vone_ref[...]
    for h in range(MLA_HEADS):
        sl = slice(h * HEAD_PAD, (h + 1) * HEAD_PAD)
        q_ref[0, h] = (q1[:, sl] * qcos + q2[:, sl] * qsin).astype(BF16)
        k_ref[0, h] = (k1[:, sl] + krope).astype(BF16)
        v_ref[0, h] = v[:, sl].astype(BF16)


def _qkv(small, tabs, gq, wq, wqs, gkv, wk, wv, vone, tm):
    b, s, ns = small.shape
    tab_spec = pl.BlockSpec((tm, LANES), lambda i, j: (j, 0))
    head_major = jax.ShapeDtypeStruct((b, MLA_HEADS, s, HEAD_PAD), BF16)
    out_spec = pl.BlockSpec((1, MLA_HEADS, tm, HEAD_PAD), lambda i, j: (i, 0, j, 0))
    return pl.pallas_call(
        _qkv_kernel,
        out_shape=(head_major, head_major, head_major),
        grid=(b, s // tm),
        in_specs=[pl.BlockSpec((1, tm, ns), lambda i, j: (i, j, 0)),
                  tab_spec, tab_spec, tab_spec, tab_spec,
                  _resident(gq.shape), _resident(wq.shape), _resident(wqs.shape),
                  _resident(gkv.shape), _resident(wk.shape), _resident(wv.shape),
                  _resident(vone.shape)],
        out_specs=(out_spec, out_spec, out_spec),
        compiler_params=_cparams(2),
        name="qkv",
    )(small, *tabs, gq, wq, wqs, gkv, wk, wv, vone)


ATTN_HEADS_PER_STEP = 8


def _attn_kernel(q_ref, k_ref, v_ref, o_ref, s_ref, m_ref, *, tk):
    s = k_ref.shape[2]
    tq = q_ref.shape[2]
    npair = q_ref.shape[1] // 2
    nk = s // tk
    lane = lax.broadcasted_iota(jnp.int32, (tq, LANES), 1)

    def qk_pass(pp, slot):
        for e in range(2):
            q = q_ref[0, 2 * pp + e]
            mpart = jnp.full((tq, LANES), -jnp.inf, F32)
            for j in range(nk):
                sc = lax.dot_general(q, k_ref[0, 2 * pp + e, j * tk:(j + 1) * tk, :],
                                     (((1,), (1,)), ((), ())), preferred_element_type=F32)
                s_ref[slot, e, :, j * tk:(j + 1) * tk] = sc
                for c in range(tk // LANES):
                    mpart = jnp.maximum(mpart, sc[:, c * LANES:(c + 1) * LANES])
            m_ref[slot, e] = jnp.broadcast_to(jnp.max(mpart, axis=-1, keepdims=True), (tq, LANES))

    def pv_pass(pp, slot):
        out = None
        for e in range(2):
            mb = m_ref[slot, e]
            acc = jnp.zeros((tq, LANES), F32)
            for j in range(nk):
                cols = [jnp.exp2(s_ref[slot, e, :, j * tk + c * LANES:j * tk + (c + 1) * LANES]
                                 - mb).astype(BF16) for c in range(tk // LANES)]
                acc = acc + jnp.dot(jnp.concatenate(cols, axis=1),
                                    v_ref[0, 2 * pp + e, j * tk:(j + 1) * tk, :],
                                    preferred_element_type=F32)
            ones_lane = V_HEAD if e == 0 else 0
            is_value = (lane < V_HEAD) if e == 0 else (lane >= V_HEAD)
            o = jnp.where(is_value, acc / acc[:, ones_lane:ones_lane + 1], 0.0)
            out = o if out is None else out + o
        o_ref[0, pp] = out.astype(BF16)

    qk_pass(0, 0)

    def body(pp, carry):
        slot = pp % 2
        pv_pass(pp, slot)
        qk_pass(pp + 1, 1 - slot)
        return carry

    lax.fori_loop(0, npair - 1, body, 0)
    pv_pass(npair - 1, (npair - 1) % 2)


def _attn(q, k, v, tq, tk):
    b, nh, s, _ = q.shape
    hg = ATTN_HEADS_PER_STEP
    kv_spec = pl.BlockSpec((1, hg, s, HEAD_PAD), lambda i, h, j: (i, h, 0, 0))
    return pl.pallas_call(
        functools.partial(_attn_kernel, tk=tk),
        out_shape=jax.ShapeDtypeStruct((b, nh // 2, s, 2 * V_HEAD), BF16),
        grid=(b, nh // hg, s // tq),
        in_specs=[pl.BlockSpec((1, hg, tq, HEAD_PAD), lambda i, h, j: (i, h, j, 0)),
                  kv_spec, kv_spec],
        out_specs=pl.BlockSpec((1, hg // 2, tq, 2 * V_HEAD), lambda i, h, j: (i, h, j, 0)),
        scratch_shapes=[pltpu.VMEM((2, 2, tq, s), F32), pltpu.VMEM((2, 2, tq, LANES), F32)],
        compiler_params=_cparams(3),
        name="attn",
    )(q, k, v)


def _pair_expand(cols, low_half):
    return jnp.concatenate([jnp.where(low_half, cols[0], cols[1]),
                            jnp.where(low_half, cols[2], cols[3])], axis=1)


def _ssd_kernel(z_ref, xbc_ref, dtt_ref, cw_ref, cbias_ref, dtb_ref, alog_ref, dsk_ref, gn_ref,
                o_ref, xs_ref, bt_ref, cs_ref, cb_ref, acc_ref, col_ref, row_ref, h_ref):
    s = xbc_ref.shape[1]
    q = SSD_CHUNK
    n = SSD_STATE
    nc = s // q
    nh2 = 2 * HPG

    a2 = -jnp.exp(alog_ref[0]) * LOG2E
    dt = jax.nn.softplus(dtt_ref[0] + dtb_ref[0][None]).reshape(nc * nh2, q)
    pre = suf = (dt.reshape(nc, nh2, q) * a2[None]).reshape(nc * nh2, q)
    lane = lax.broadcasted_iota(jnp.int32, (nc * nh2, q), 1)
    is_fwd = (lax.broadcasted_iota(jnp.int32, (nc * nh2, q), 0) % nh2) < HPG
    k = 1
    while k < q:
        pre = pre + jnp.where(lane >= k, pltpu.roll(pre, k, axis=1), 0.0)
        suf = suf + jnp.where(lane < q - k, pltpu.roll(suf, q - k, axis=1), 0.0)
        k *= 2
    cum = jnp.where(is_fwd, pre, suf)
    end = jnp.where(is_fwd, jnp.broadcast_to(cum[:, q - 1:q], cum.shape),
                    jnp.broadcast_to(cum[:, 0:1], cum.shape))
    row_ref[:, 0:nh2, :] = (cum - jnp.log2(dt)).reshape(nc, nh2, q)
    row_ref[:, nh2:2 * nh2, :] = (dt * jnp.exp2(end - cum)).reshape(nc, nh2, q)
    row_ref[:, 2 * nh2:3 * nh2, :] = jnp.exp2(end).reshape(nc, nh2, q)
    cum3 = cum.reshape(nc, nh2, q)
    zeros_pad = jnp.zeros((q - nh2, q), F32)
    for c in range(nc):
        col_ref[c] = jnp.concatenate([cum3[c], zeros_pad], axis=0).T

    w = cw_ref[...]
    bias = cbias_ref[...]
    dsk = dsk_ref[0]
    pad = CONV_WIDTH // 2

    win = 2 * q
    sh_r = lax.broadcasted_iota(jnp.int32, (q, win), 0)
    sh_c = lax.broadcasted_iota(jnp.int32, (q, win), 1)
    taps = [t for t in range(CONV_WIDTH) if t != pad]
    shifts = jnp.concatenate([jnp.where(sh_c == sh_r + (HALO - pad + t), 1.0, 0.0).astype(BF16)
                              for t in taps], axis=0)

    zero_halo = jnp.zeros((HALO, GW + 2 * n), BF16)
    tail = jnp.zeros((win - q - 2 * HALO, GW + 2 * n), BF16)

    col_halves = (slice(0, GW), slice(GW, 2 * GW))

    def conv_shift(c):
        r0 = pl.multiple_of(c * q, q)
        main = xbc_ref[0, pl.ds(r0, q), :]
        p0 = pl.multiple_of(jnp.maximum(r0 - HALO, 0), HALO)
        n0 = pl.multiple_of(jnp.minimum(r0 + q, s - HALO), HALO)
        prev = jnp.where(c > 0, xbc_ref[0, pl.ds(p0, HALO), :], zero_halo)
        nxt = jnp.where(c < nc - 1, xbc_ref[0, pl.ds(n0, HALO), :], zero_halo)
        window = jnp.concatenate([prev, main, nxt, tail], axis=0)
        return main, [jnp.dot(shifts, window[:, sl], preferred_element_type=F32)
                      for sl in col_halves]

    def conv_finish(c, main, shifted):
        halves = []
        for sl, sh in zip(col_halves, shifted):
            out = bias[:, sl] + main[:, sl].astype(F32) * w[pad:pad + 1, sl]
            for i, t in enumerate(taps):
                out = out + sh[i * q:(i + 1) * q] * w[t:t + 1, sl]
            halves.append(_silu(out))
        xc = halves[0]
        bconv = halves[1][:, 0:n]
        cmat = halves[1][:, n:2 * n].astype(BF16)
        xs_ref[c] = xc.astype(BF16)
        acc_ref[c] = xc * dsk
        cs_ref[c] = cmat
        bt_ref[c] = bconv.T
        cb_ref[c] = lax.dot_general(cmat, bconv.astype(BF16), (((1,), (1,)), ((), ())),
                                    preferred_element_type=F32)

    conv_group = 4 if nc % 4 == 0 else 2

    def conv_body(i, carry):
        cs = [i * conv_group + u for u in range(conv_group)]
        staged = [conv_shift(c) for c in cs]
        for c, (main, shifted) in zip(cs, staged):
            conv_finish(c, main, shifted)
        return carry

    lax.fori_loop(0, nc // conv_group, conv_body, 0)

    h_ref[...] = jnp.zeros_like(h_ref)
    row_i = lax.broadcasted_iota(jnp.int32, (q, q), 0)
    col_i = lax.broadcasted_iota(jnp.int32, (q, q), 1)
    low_half = col_i < SSD_HEAD_DIM
    low_row = low_half[0:1, :]
    lane_head = lax.broadcasted_iota(jnp.int32, (q, GW), 1) // SSD_HEAD_DIM
    masks = (row_i >= col_i, row_i <= col_i)

    def scan_local(c, d):
        xb = xs_ref[c]
        colblk = col_ref[c]
        cb = cb_ref[c]
        bt = bt_ref[c]
        zero_b = jnp.zeros_like(xb)
        row = lambda kind, j: row_ref[c, kind * nh2 + d * HPG + j:kind * nh2 + d * HPG + j + 1, :]
        cum_b = [jnp.broadcast_to(colblk[:, d * HPG + j:d * HPG + j + 1], (q, LANES))
                 for j in range(HPG)]
        ydiag = upd = None
        for jp in range(HPG // 2):
            ms, bts, xsel = [], [], []
            for j in (2 * jp, 2 * jp + 1):
                dec = jnp.exp2(jnp.where(masks[d], cum_b[j] - row(0, j), -jnp.inf))
                ms.append((cb * dec).astype(BF16))
                bts.append((bt * row(1, j)).astype(BF16))
                xsel.append(jnp.where(lane_head == j, xb, zero_b))
            xpair = jnp.concatenate(xsel, axis=0)
            t1 = jnp.dot(jnp.concatenate(ms, axis=1), xpair, preferred_element_type=F32)
            t2 = jnp.dot(jnp.concatenate(bts, axis=1), xpair, preferred_element_type=F32)
            ydiag = t1 if ydiag is None else ydiag + t1
            upd = t2 if upd is None else upd + t2
        in_scale = jnp.exp2(_pair_expand(cum_b, low_half))
        decay = _pair_expand([row(2, j) for j in range(HPG)], low_row)
        return ydiag, upd, in_scale, decay

    def direction(c, d):
        ydiag, upd, in_scale, decay = scan_local(c, d)
        hprev = h_ref[d]
        yoff = jnp.dot(cs_ref[c], hprev.astype(BF16), preferred_element_type=F32)
        acc_ref[c] += ydiag + yoff * in_scale
        h_ref[d] = hprev * decay + upd

    def scan_body(i, carry):
        direction(i, 0)
        direction(nc - 1 - i, 1)
        return carry

    lax.fori_loop(0, nc, scan_body, 0, unroll=4 if nc % 4 == 0 else 2)

    gn = gn_ref[...]

    def out_body(c, carry):
        r0 = pl.multiple_of(c * q, q)
        y = acc_ref[c] * _silu(z_ref[0, pl.ds(r0, q), :].astype(F32))
        o_ref[0, pl.ds(r0, q), :] = (_rms(y) * gn).astype(BF16)
        return carry

    lax.fori_loop(0, nc, out_body, 0, unroll=2)


def _ssd(main, dtt, cw, cb, dtb, alog, dsk, gn):
    b, s, _ = main.shape
    g = SSD_GROUPS
    n = SSD_STATE
    q = SSD_CHUNK
    nc = s // q
    gc = GW + 2 * n
    xbc_blk0 = (g * GW) // gc
    return pl.pallas_call(
        _ssd_kernel,
        out_shape=jax.ShapeDtypeStruct((b, s, g * GW), BF16),
        grid=(b, g),
        in_specs=[pl.BlockSpec((1, s, GW), lambda i, j: (i, 0, j)),
                  pl.BlockSpec((1, s, gc), lambda i, j: (i, 0, xbc_blk0 + j)),
                  pl.BlockSpec((1, nc, 2 * HPG, q), lambda i, j: (i, 0, j, 0)),
                  pl.BlockSpec((CONV_WIDTH, gc), lambda i, j: (0, j)),
                  pl.BlockSpec((1, gc), lambda i, j: (0, j)),
                  pl.BlockSpec((1, 2 * HPG, LANES), lambda i, j: (j, 0, 0)),
                  pl.BlockSpec((1, 2 * HPG, LANES), lambda i, j: (j, 0, 0)),
                  pl.BlockSpec((1, 1, GW), lambda i, j: (j, 0, 0)),
                  pl.BlockSpec((1, GW), lambda i, j: (0, j))],
        out_specs=pl.BlockSpec((1, s, GW), lambda i, j: (i, 0, j)),
        scratch_shapes=[pltpu.VMEM((nc, q, GW), BF16),
                        pltpu.VMEM((nc, n, q), F32),
                        pltpu.VMEM((nc, q, n), BF16),
                        pltpu.VMEM((nc, q, q), F32),
                        pltpu.VMEM((nc, q, GW), F32),
                        pltpu.VMEM((nc, q, LANES), F32),
                        pltpu.VMEM((nc, 6 * HPG, q), F32),
                        pltpu.VMEM((2, n, GW), F32)],
        compiler_params=_cparams(2),
        name="ssd",
    )(main, main, dtt, cw, cb, dtb, alog, dsk, gn)


def _merge_kernel(yg_ref, at_ref, gt_ref, x_ref, g1_ref, wa_ref, wb_ref, wo_ref, o_ref):
    d = x_ref.shape[2]
    ya = jnp.dot(yg_ref[0], wa_ref[...], preferred_element_type=F32)
    attn = jnp.concatenate([at_ref[0, p] for p in range(at_ref.shape[1])], axis=1)
    yb = jnp.dot(attn, wb_ref[...], preferred_element_type=F32)
    gates = jax.nn.sigmoid(gt_ref[0].astype(F32))
    mixed = (gates[:, 0:d] * ya + gates[:, d:2 * d] * yb).astype(BF16)
    o_ref[0] = x_ref[0] + g1_ref[0] * jnp.dot(mixed, wo_ref[...], preferred_element_type=F32)


def _merge(yg, attn, main, x, gate1, wa, wb, wo, tm):
    b, s, d = x.shape
    gate_blk = (main.shape[2] - 2 * d) // (2 * d)
    return pl.pallas_call(
        _merge_kernel,
        out_shape=jax.ShapeDtypeStruct((b, s, d), F32),
        grid=(b, s // tm),
        in_specs=[pl.BlockSpec((1, tm, yg.shape[2]), lambda i, j: (i, j, 0)),
                  pl.BlockSpec((1, attn.shape[1], tm, attn.shape[3]), lambda i, j: (i, 0, j, 0)),
                  pl.BlockSpec((1, tm, 2 * d), lambda i, j: (i, j, gate_blk)),
                  pl.BlockSpec((1, tm, d), lambda i, j: (i, j, 0)),
                  pl.BlockSpec((1, 1, d), lambda i, j: (i, 0, 0)),
                  _resident(wa.shape), _resident(wb.shape), _resident(wo.shape)],
        out_specs=pl.BlockSpec((1, tm, d), lambda i, j: (i, j, 0)),
        compiler_params=_cparams(2),
        name="merge",
    )(yg, attn, main, x, gate1, wa, wb, wo)


def _mlp_kernel(x_ref, sc_ref, sh_ref, g2_ref, gn_ref, gf_ref, w1_ref, w2_ref, o_ref,
                *, ff_chunk):
    x = x_ref[0]
    h = _rms(x) * gn_ref[...]
    hb = (h * (1.0 + sc_ref[0]) + sh_ref[0]).astype(BF16)
    acc = None
    for j in range(w1_ref.shape[1] // ff_chunk):
        sl = slice(j * ff_chunk, (j + 1) * ff_chunk)
        u = jnp.maximum(jnp.dot(hb, w1_ref[:, sl], preferred_element_type=F32), 0.0)
        term = jnp.dot((u * u).astype(BF16), w2_ref[sl, :], preferred_element_type=F32)
        acc = term if acc is None else acc + term
    y = x + g2_ref[0] * acc
    o_ref[0] = _rms(y) * gf_ref[...]


def _mlp(x, scale, shift, gate2, gn, gf, w1, w2, tm):
    b, s, d = x.shape
    vec = pl.BlockSpec((1, 1, d), lambda i, j: (i, 0, 0))
    return pl.pallas_call(
        functools.partial(_mlp_kernel, ff_chunk=1024),
        out_shape=jax.ShapeDtypeStruct((b, s, d), F32),
        grid=(b, s // tm),
        in_specs=[pl.BlockSpec((1, tm, d), lambda i, j: (i, j, 0)), vec, vec, vec,
                  _resident(gn.shape), _resident(gf.shape),
                  _resident(w1.shape), _resident(w2.shape)],
        out_specs=pl.BlockSpec((1, tm, d), lambda i, j: (i, j, 0)),
        compiler_params=_cparams(2),
        name="mlp",
    )(x, scale, shift, gate2, gn, gf, w1, w2)


def _rope_tables(s):
    inv = 1.0 / (ROPE_THETA ** (jnp.arange(0, QK_ROPE, 2, dtype=F32) / QK_ROPE))
    ang = jnp.arange(s, dtype=F32)[:, None] * inv[None, :]
    cos, sin = jnp.cos(ang), jnp.sin(ang)
    zn = jnp.zeros((s, QK_NOPE), F32)
    zp = jnp.zeros((s, HEAD_PAD - QK_NOPE - QK_ROPE), F32)
    kcos = jnp.concatenate([zn, cos, cos, zp], axis=1)
    ksin = jnp.concatenate([zn, -sin, sin, zp], axis=1)
    scale = (QK_NOPE + QK_ROPE) ** -0.5 * LOG2E
    qcos = jnp.concatenate([jnp.ones((s, QK_NOPE), F32), cos, cos, zp], axis=1) * scale
    qsin = ksin * scale
    return qcos, qsin, kcos, ksin


def _group_xbc(a, d_inner):
    lead = a.shape[:-1]
    nbc = SSD_GROUPS * SSD_STATE
    x = a[..., :d_inner].reshape(lead + (SSD_GROUPS, GW))
    bm = a[..., d_inner:d_inner + nbc].reshape(lead + (SSD_GROUPS, SSD_STATE))
    cm = a[..., d_inner + nbc:].reshape(lead + (SSD_GROUPS, SSD_STATE))
    return jnp.concatenate([x, bm, cm], axis=-1).reshape(lead + (-1,))


def _prep_weights(w_in, w_q_b, w_kv_b, d_inner):
    d = w_in.shape[0]
    conv_ch = d_inner + 2 * SSD_GROUPS * SSD_STATE
    n_heads = d_inner // SSD_HEAD_DIM
    o_z, o_x = 0, d_inner
    o_dt = o_x + conv_ch
    o_q = o_dt + 2 * n_heads
    o_kv = o_q + Q_LORA
    o_g = o_kv + KV_LORA + QK_ROPE
    half = QK_ROPE // 2
    w_main = jnp.concatenate([w_in[:, o_z:o_x], _group_xbc(w_in[:, o_x:o_dt], d_inner),
                              w_in[:, o_g:]], axis=1).astype(BF16)
    kr = w_in[:, o_kv + KV_LORA:o_g]
    kr_sw = jnp.concatenate([kr[:, half:], kr[:, :half]], axis=1)
    zl = jnp.zeros((d, QK_NOPE), F32)
    zr = jnp.zeros((d, HEAD_PAD - QK_NOPE - QK_ROPE), F32)
    w_small = jnp.concatenate(
        [w_in[:, o_q:o_kv], w_in[:, o_kv:o_kv + KV_LORA], zl, kr, zr, zl, kr_sw, zr],
        axis=1).astype(BF16)
    w_dt = w_in[:, o_dt:o_q].T.reshape(2, SSD_GROUPS, HPG, d)
    w_dt = jnp.transpose(w_dt, (1, 0, 2, 3)).reshape(2 * n_heads, d).astype(BF16)

    lq = w_q_b.shape[0]
    wq3 = w_q_b.reshape(lq, MLA_HEADS, QK_NOPE + QK_ROPE)
    q_nope, q_r = wq3[..., :QK_NOPE], wq3[..., QK_NOPE:]
    q_rsw = jnp.concatenate([q_r[..., half:], q_r[..., :half]], axis=-1)
    zq = jnp.zeros((lq, MLA_HEADS, HEAD_PAD - QK_NOPE - QK_ROPE), F32)
    wq = jnp.concatenate([q_nope, q_r, zq], axis=-1).reshape(lq, -1).astype(BF16)
    wqs = jnp.concatenate([jnp.zeros_like(q_nope), q_rsw, zq], axis=-1).reshape(lq, -1).astype(BF16)
    lk = w_kv_b.shape[0]
    wkv3 = w_kv_b.reshape(lk, MLA_HEADS, QK_NOPE + V_HEAD)
    zk = jnp.zeros((lk, MLA_HEADS, HEAD_PAD - QK_NOPE), F32)
    wk = jnp.concatenate([wkv3[..., :QK_NOPE], zk], axis=-1).reshape(lk, -1).astype(BF16)
    v4 = wkv3[..., QK_NOPE:].reshape(lk, MLA_HEADS // 2, 2, V_HEAD)
    zv = jnp.zeros((lk, MLA_HEADS // 2, V_HEAD), F32)
    wv = jnp.stack([jnp.concatenate([v4[:, :, 0], zv], axis=-1),
                    jnp.concatenate([zv, v4[:, :, 1]], axis=-1)], axis=2)
    wv = wv.reshape(lk, MLA_HEADS * HEAD_PAD).astype(BF16)
    pair_ones = jnp.zeros((2 * HEAD_PAD,), F32).at[V_HEAD].set(1.0).at[HEAD_PAD].set(1.0)
    vone = jnp.tile(pair_ones, MLA_HEADS // 2).reshape(1, MLA_HEADS * HEAD_PAD)
    return w_main, w_small, w_dt, wq, wqs, wk, wv, vone


def _group_rows(v):
    f, bwd = v
    g = jnp.concatenate([f.reshape(SSD_GROUPS, HPG), bwd.reshape(SSD_GROUPS, HPG)], axis=1)
    return jnp.broadcast_to(g[:, :, None], (SSD_GROUPS, 2 * HPG, LANES)).astype(F32)


def kernel(x_prompt, x_sample, c_prompt, c_sample, w_ada, b_ada, g_norm1, w_in, conv_w,
           conv_b, dt_bias_fwd, dt_bias_bwd, a_log_fwd, a_log_bwd, d_skip, g_ssd_norm,
           w_ssd_out, g_q_norm, w_q_b, g_kv_norm, w_kv_b, w_mla_out, w_o, g_norm2,
           w_mlp_in, w_mlp_out, g_final):
    assert w_ada.shape[0] == 1, "single layer"
    d = x_prompt.shape[2]
    d_inner = w_ssd_out.shape[1]
    w_main, w_small, w_dt, wq, wqs, wk, wv, vone = _prep_weights(
        w_in[0], w_q_b[0], w_kv_b[0], d_inner)
    wa, wb, wo = (w_ssd_out[0].astype(BF16), w_mla_out[0].astype(BF16), w_o[0].astype(BF16))
    w1, w2 = w_mlp_in[0].astype(BF16), w_mlp_out[0].astype(BF16)
    row = lambda v: v.reshape(1, -1).astype(F32)
    dtb = _group_rows((dt_bias_fwd[0], dt_bias_bwd[0]))
    alog = _group_rows((a_log_fwd[0], a_log_bwd[0]))
    dsk = jnp.repeat(d_skip[0].astype(F32), SSD_HEAD_DIM).reshape(SSD_GROUPS, 1, GW)
    cw = _group_xbc(conv_w[0].astype(F32), d_inner)
    cbias = _group_xbc(row(conv_b[0]), d_inner)

    nb = c_prompt.shape[0]
    ada = _ada(jnp.concatenate([c_prompt, c_sample], axis=0), w_ada[0], b_ada[0])

    def trunk(x, ada_rows):
        b, s, _ = x.shape
        mods = [ada_rows[:, None, i * d:(i + 1) * d] for i in range(N_ADA)]
        shift1, scale1, gate1, shift2, scale2, gate2 = mods
        tm = min(512, s)
        main, small, dtt = _inproj(x, scale1, shift1, row(g_norm1[0]), w_main, w_small, w_dt, tm)
        q, k, v = _qkv(small, _rope_tables(s), row(g_q_norm[0]), wq, wqs,
                       row(g_kv_norm[0]), wk, wv, vone, tm)
        attn = _attn(q, k, v, min(256, s), min(512, s))
        yg = _ssd(main, dtt, cw, cbias, dtb, alog, dsk, row(g_ssd_norm[0]))
        x1 = _merge(yg, attn, main, x, gate1, wa, wb, wo, tm)
        return _mlp(x1, scale2, shift2, gate2, row(g_norm2[0]), row(g_final), w1, w2, tm)

    return trunk(x_prompt, ada[:nb]), trunk(x_sample, ada[nb:])
```

```python
import functools
import math

import jax
import jax.numpy as jnp
from jax import lax
from jax.experimental import pallas as pl
from jax.experimental.pallas import tpu as pltpu

F32 = jnp.float32
BF16 = jnp.bfloat16

SSD_HEAD_DIM = 64
SSD_GROUPS = 8
SSD_STATE = 128
SSD_CHUNK = 128
CONV_WIDTH = 5
MLA_HEADS = 16
Q_LORA = 384
KV_LORA = 256
QK_NOPE = 64
QK_ROPE = 32
V_HEAD = 64
ROPE_THETA = 10000.0
N_ADA = 6
EPS = 1e-6
LOG2E = math.log2(math.e)

LANES = 128
HEAD_PAD = 128
VMEM_LIMIT = 56 * 1024 * 1024

HPG = 4
GW = HPG * SSD_HEAD_DIM
HALO = 16

CKV_OFF = Q_LORA
KR_OFF = CKV_OFF + KV_LORA
KRS_OFF = KR_OFF + LANES
SMALL_W = KRS_OFF + LANES


def _cparams(n_axes):
    return pltpu.CompilerParams(
        dimension_semantics=("arbitrary",) * n_axes, vmem_limit_bytes=VMEM_LIMIT)


def _resident(shape):
    nd = len(shape)
    return pl.BlockSpec(shape, lambda *_: (0,) * nd, pipeline_mode=pl.Buffered(1))


def _rms(x):
    return x * lax.rsqrt(jnp.mean(x * x, axis=-1, keepdims=True) + EPS)


def _silu(x):
    return x * jax.nn.sigmoid(x)


def _ada_kernel(c_ref, w_ref, b_ref, o_ref):
    s = _silu(c_ref[...])
    o_ref[...] = jnp.dot(s, w_ref[...], precision=lax.Precision.HIGHEST,
                         preferred_element_type=F32) + b_ref[...]


def _ada(c, w, b):
    n, d = c.shape
    dout = w.shape[1]
    tn = d
    return pl.pallas_call(
        _ada_kernel,
        out_shape=jax.ShapeDtypeStruct((n, dout), F32),
        grid=(dout // tn,),
        in_specs=[pl.BlockSpec((n, d), lambda j: (0, 0)),
                  pl.BlockSpec((d, tn), lambda j: (0, j)),
                  pl.BlockSpec((1, tn), lambda j: (0, j))],
        out_specs=pl.BlockSpec((n, tn), lambda j: (0, j)),
        compiler_params=_cparams(1),
        name="ada",
    )(c, w, b.reshape(1, dout))


def _inproj_kernel(x_ref, sc_ref, sh_ref, g_ref, wm_ref, ws_ref, wdt_ref,
                   main_ref, small_ref, dtt_ref, *, col_chunk):
    h = _rms(x_ref[0]) * g_ref[...]
    h = h * (1.0 + sc_ref[0]) + sh_ref[0]
    hb = h.astype(BF16)
    for j in range(wm_ref.shape[1] // col_chunk):
        sl = slice(j * col_chunk, (j + 1) * col_chunk)
        main_ref[0, :, sl] = jnp.dot(hb, wm_ref[:, sl], preferred_element_type=F32).astype(BF16)
    small_ref[0] = jnp.dot(hb, ws_ref[...], preferred_element_type=F32)
    dtt = lax.dot_general(wdt_ref[...], hb, (((1,), (1,)), ((), ())), preferred_element_type=F32)
    for c in range(dtt_ref.shape[1]):
        dtt_ref[0, c] = dtt[:, c * SSD_CHUNK:(c + 1) * SSD_CHUNK]


def _inproj(x, scale, shift, g, wm, ws, wdt, tm):
    b, s, d = x.shape
    nm, ns, nh = wm.shape[1], ws.shape[1], wdt.shape[0]
    return pl.pallas_call(
        functools.partial(_inproj_kernel, col_chunk=2048),
        out_shape=(jax.ShapeDtypeStruct((b, s, nm), BF16),
                   jax.ShapeDtypeStruct((b, s, ns), F32),
                   jax.ShapeDtypeStruct((b, s // SSD_CHUNK, nh, SSD_CHUNK), F32)),
        grid=(b, s // tm),
        in_specs=[pl.BlockSpec((1, tm, d), lambda i, j: (i, j, 0)),
                  pl.BlockSpec((1, 1, d), lambda i, j: (i, 0, 0)),
                  pl.BlockSpec((1, 1, d), lambda i, j: (i, 0, 0)),
                  _resident(g.shape), _resident(wm.shape), _resident(ws.shape),
                  _resident(wdt.shape)],
        out_specs=(pl.BlockSpec((1, tm, nm), lambda i, j: (i, j, 0)),
                   pl.BlockSpec((1, tm, ns), lambda i, j: (i, j, 0)),
                   pl.BlockSpec((1, tm // SSD_CHUNK, nh, SSD_CHUNK), lambda i, j: (i, j, 0, 0))),
        compiler_params=_cparams(2),
        name="in_proj",
    )(x, scale, shift, g, wm, ws, wdt)


def _qkv_kernel(sm_ref, qc_ref, qs_ref, kc_ref, ks_ref, gq_ref, wq_ref,
                gkv_ref, wk_ref, wv_ref, vone_ref, q_ref, k_ref, v_ref):
    sm = sm_ref[0]
    qn = (_rms(sm[:, 0:Q_LORA]) * gq_ref[...]).astype(BF16)
    cn = (_rms(sm[:, CKV_OFF:KR_OFF]) * gkv_ref[...]).astype(BF16)
    krope = sm[:, KR_OFF:KRS_OFF] * kc_ref[...] + sm[:, KRS_OFF:SMALL_W] * ks_ref[...]
    qcos, qsin = qc_ref[...], qs_ref[...]
    q1 = jnp.dot(qn, wq_ref[...], preferred_element_type=F32)
    k1 = jnp.dot(cn, wk_ref[...], preferred_element_type=F32)
    v = jnp.dot(cn, wv_ref[...], preferred_element_type=F32) + vone_ref[...]
    half = QK_ROPE // 2
    lane = lax.broadcasted_iota(jnp.int32, (sm.shape[0], HEAD_PAD), 1)
    first_half = lane < QK_NOPE + half
    for h in range(MLA_HEADS):
        sl = slice(h * HEAD_PAD, (h + 1) * HEAD_PAD)
        t = q1[:, sl]
        swapped = jnp.where(first_half, pltpu.roll(t, HEAD_PAD - half, axis=1),
                            pltpu.roll(t, half, axis=1))
        q_ref[0, h] = (t * qcos + swapped * qsin).astype(BF16)
        k_ref[0, h] = (k1[:, sl] + krope).astype(BF16)
        v_ref[0, h] = v[:, sl].astype(BF16)


def _qkv(small, tabs, gq, wq, gkv, wk, wv, vone, tm):
    b, s, ns = small.shape
    tab_spec = pl.BlockSpec((tm, LANES), lambda i, j: (j, 0))
    head_major = jax.ShapeDtypeStruct((b, MLA_HEADS, s, HEAD_PAD), BF16)
    out_spec = pl.BlockSpec((1, MLA_HEADS, tm, HEAD_PAD), lambda i, j: (i, 0, j, 0))
    return pl.pallas_call(
        _qkv_kernel,
        out_shape=(head_major, head_major, head_major),
        grid=(b, s // tm),
        in_specs=[pl.BlockSpec((1, tm, ns), lambda i, j: (i, j, 0)),
                  tab_spec, tab_spec, tab_spec, tab_spec,
                  _resident(gq.shape), _resident(wq.shape),
                  _resident(gkv.shape), _resident(wk.shape), _resident(wv.shape),
                  _resident(vone.shape)],
        out_specs=(out_spec, out_spec, out_spec),
        compiler_params=_cparams(2),
        name="qkv",
    )(small, *tabs, gq, wq, gkv, wk, wv, vone)


ATTN_HEADS_PER_STEP = 8
ATTN_LOGIT_ELEMS = 1 << 20


def _attn_kernel(q_ref, k_ref, v_ref, o_ref, s_ref, m_ref, *, tk):
    s = k_ref.shape[2]
    tq = q_ref.shape[2]
    npair = q_ref.shape[1] // 2
    nk = s // tk
    lane = lax.broadcasted_iota(jnp.int32, (tq, LANES), 1)

    def qk_pass(pp, slot):
        for e in range(2):
            q = q_ref[0, 2 * pp + e]
            mpart = jnp.full((tq, LANES), -jnp.inf, F32)
            for j in range(nk):
                sc = lax.dot_general(q, k_ref[0, 2 * pp + e, j * tk:(j + 1) * tk, :],
                                     (((1,), (1,)), ((), ())), preferred_element_type=F32)
                s_ref[slot, e, :, j * tk:(j + 1) * tk] = sc
                for c in range(tk // LANES):
                    mpart = jnp.maximum(mpart, sc[:, c * LANES:(c + 1) * LANES])
            m_ref[slot, e] = jnp.broadcast_to(jnp.max(mpart, axis=-1, keepdims=True), (tq, LANES))

    def pv_pass(pp, slot):
        out = None
        for e in range(2):
            mb = m_ref[slot, e]
            acc = jnp.zeros((tq, LANES), F32)
            for j in range(nk):
                cols = [jnp.exp2(s_ref[slot, e, :, j * tk + c * LANES:j * tk + (c + 1) * LANES]
                                 - mb).astype(BF16) for c in range(tk // LANES)]
                acc = acc + jnp.dot(jnp.concatenate(cols, axis=1),
                                    v_ref[0, 2 * pp + e, j * tk:(j + 1) * tk, :],
                                    preferred_element_type=F32)
            ones_lane = V_HEAD if e == 0 else 0
            is_value = (lane < V_HEAD) if e == 0 else (lane >= V_HEAD)
            o = jnp.where(is_value, acc / acc[:, ones_lane:ones_lane + 1], 0.0)
            out = o if out is None else out + o
        o_ref[0, pp] = out.astype(BF16)

    qk_pass(0, 0)

    def body(pp, carry):
        slot = pp % 2
        pv_pass(pp, slot)
        qk_pass(pp + 1, 1 - slot)
        return carry

    lax.fori_loop(0, npair - 1, body, 0)
    pv_pass(npair - 1, (npair - 1) % 2)


def _attn(q, k, v, tq, tk):
    b, nh, s, _ = q.shape
    hg = ATTN_HEADS_PER_STEP
    kv_spec = pl.BlockSpec((1, hg, s, HEAD_PAD), lambda i, h, j: (i, h, 0, 0))
    return pl.pallas_call(
        functools.partial(_attn_kernel, tk=tk),
        out_shape=jax.ShapeDtypeStruct((b, nh // 2, s, 2 * V_HEAD), BF16),
        grid=(b, nh // hg, s // tq),
        in_specs=[pl.BlockSpec((1, hg, tq, HEAD_PAD), lambda i, h, j: (i, h, j, 0)),
                  kv_spec, kv_spec],
        out_specs=pl.BlockSpec((1, hg // 2, tq, 2 * V_HEAD), lambda i, h, j: (i, h, j, 0)),
        scratch_shapes=[pltpu.VMEM((2, 2, tq, s), F32), pltpu.VMEM((2, 2, tq, LANES), F32)],
        compiler_params=_cparams(3),
        name="attn",
    )(q, k, v)


def _pair_expand(cols, low_half):
    return jnp.concatenate([jnp.where(low_half, cols[0], cols[1]),
                            jnp.where(low_half, cols[2], cols[3])], axis=1)


def _ssd_kernel(z_ref, xbc_ref, dtt_ref, cw_ref, cbias_ref, dtb_ref, alog_ref, dsk_ref, gn_ref,
                o_ref, xp_ref, bt_ref, cs_ref, cb_ref, acc_ref, col_ref, row_ref, h_ref):
    s = xbc_ref.shape[1]
    q = SSD_CHUNK
    n = SSD_STATE
    nc = s // q
    nh2 = 2 * HPG

    a2 = -jnp.exp(alog_ref[0]) * LOG2E
    dt = jax.nn.softplus(dtt_ref[0] + dtb_ref[0][None]).reshape(nc * nh2, q)
    pre = suf = (dt.reshape(nc, nh2, q) * a2[None]).reshape(nc * nh2, q)
    lane = lax.broadcasted_iota(jnp.int32, (nc * nh2, q), 1)
    is_fwd = (lax.broadcasted_iota(jnp.int32, (nc * nh2, q), 0) % nh2) < HPG
    k = 1
    while k < q:
        pre = pre + jnp.where(lane >= k, pltpu.roll(pre, k, axis=1), 0.0)
        suf = suf + jnp.where(lane < q - k, pltpu.roll(suf, q - k, axis=1), 0.0)
        k *= 2
    cum = jnp.where(is_fwd, pre, suf)
    end = jnp.where(is_fwd, jnp.broadcast_to(cum[:, q - 1:q], cum.shape),
                    jnp.broadcast_to(cum[:, 0:1], cum.shape))
    row_ref[:, 0:nh2, :] = (cum - jnp.log2(dt)).reshape(nc, nh2, q)
    row_ref[:, nh2:2 * nh2, :] = (dt * jnp.exp2(end - cum)).reshape(nc, nh2, q)
    row_ref[:, 2 * nh2:3 * nh2, :] = jnp.exp2(end).reshape(nc, nh2, q)
    cum3 = cum.reshape(nc, nh2, q)
    zeros_pad = jnp.zeros((q - nh2, q), F32)
    for c in range(nc):
        col_ref[c] = jnp.concatenate([cum3[c], zeros_pad], axis=0).T

    w = cw_ref[...]
    bias = cbias_ref[...]
    dsk = dsk_ref[0]
    pad = CONV_WIDTH // 2

    win = 2 * q
    sh_r = lax.broadcasted_iota(jnp.int32, (q, win), 0)
    sh_c = lax.broadcasted_iota(jnp.int32, (q, win), 1)
    taps = [t for t in range(CONV_WIDTH) if t != pad]
    shifts = jnp.concatenate([jnp.where(sh_c == sh_r + (HALO - pad + t), 1.0, 0.0).astype(BF16)
                              for t in taps], axis=0)

    zero_halo = jnp.zeros((HALO, GW + 2 * n), BF16)
    tail = jnp.zeros((win - q - 2 * HALO, GW + 2 * n), BF16)

    col_halves = (slice(0, GW), slice(GW, 2 * GW))
    lane_head = lax.broadcasted_iota(jnp.int32, (q, GW), 1) // SSD_HEAD_DIM

    def conv_shift(c):
        r0 = pl.multiple_of(c * q, q)
        main = xbc_ref[0, pl.ds(r0, q), :]
        p0 = pl.multiple_of(jnp.maximum(r0 - HALO, 0), HALO)
        n0 = pl.multiple_of(jnp.minimum(r0 + q, s - HALO), HALO)
        prev = jnp.where(c > 0, xbc_ref[0, pl.ds(p0, HALO), :], zero_halo)
        nxt = jnp.where(c < nc - 1, xbc_ref[0, pl.ds(n0, HALO), :], zero_halo)
        window = jnp.concatenate([prev, main, nxt, tail], axis=0)
        return main, [jnp.dot(shifts, window[:, sl], preferred_element_type=F32)
                      for sl in col_halves]

    def conv_finish(c, main, shifted):
        halves = []
        for sl, sh in zip(col_halves, shifted):
            out = bias[:, sl] + main[:, sl].astype(F32) * w[pad:pad + 1, sl]
            for i, t in enumerate(taps):
                out = out + sh[i * q:(i + 1) * q] * w[t:t + 1, sl]
            halves.append(_silu(out))
        xc = halves[0]
        bconv = halves[1][:, 0:n]
        cmat = halves[1][:, n:2 * n].astype(BF16)
        xb = xc.astype(BF16)
        zero_b = jnp.zeros_like(xb)
        for jp in range(HPG // 2):
            xp_ref[c, jp] = jnp.concatenate(
                [jnp.where(lane_head == j, xb, zero_b) for j in (2 * jp, 2 * jp + 1)], axis=0)
        acc_ref[c] = xc * dsk
        cs_ref[c] = cmat
        bt_ref[c] = bconv.T
        cb_ref[c] = lax.dot_general(cmat, bconv.astype(BF16), (((1,), (1,)), ((), ())),
                                    preferred_element_type=F32)

    conv_group = 4 if nc % 4 == 0 else 2

    def conv_body(i, carry):
        cs = [i * conv_group + u for u in range(conv_group)]
        staged = [conv_shift(c) for c in cs]
        for c, (main, shifted) in zip(cs, staged):
            conv_finish(c, main, shifted)
        return carry

    lax.fori_loop(0, nc // conv_group, conv_body, 0)

    h_ref[...] = jnp.zeros_like(h_ref)
    row_i = lax.broadcasted_iota(jnp.int32, (q, q), 0)
    col_i = lax.broadcasted_iota(jnp.int32, (q, q), 1)
    low_half = col_i < SSD_HEAD_DIM
    low_row = low_half[0:1, :]
    masks = (row_i >= col_i, row_i <= col_i)

    def scan_local(c, d):
        colblk = col_ref[c]
        cb = cb_ref[c]
        bt = bt_ref[c]
        row = lambda kind, j: row_ref[c, kind * nh2 + d * HPG + j:kind * nh2 + d * HPG + j + 1, :]
        cum_b = [jnp.broadcast_to(colblk[:, d * HPG + j:d * HPG + j + 1], (q, LANES))
                 for j in range(HPG)]
        ydiag = upd = None
        for jp in range(HPG // 2):
            ms, bts = [], []
            for j in (2 * jp, 2 * jp + 1):
                dec = jnp.exp2(jnp.where(masks[d], cum_b[j] - row(0, j), -jnp.inf))
                ms.append((cb * dec).astype(BF16))
                bts.append((bt * row(1, j)).astype(BF16))
            lhs = jnp.concatenate([jnp.concatenate(ms, axis=1), jnp.concatenate(bts, axis=1)],
                                  axis=0)
            both = jnp.dot(lhs, xp_ref[c, jp], preferred_element_type=F32)
            ydiag = both[0:q] if ydiag is None else ydiag + both[0:q]
            upd = both[q:q + n] if upd is None else upd + both[q:q + n]
        in_scale = jnp.exp2(_pair_expand(cum_b, low_half))
        decay = _pair_expand([row(2, j) for j in range(HPG)], low_row)
        return ydiag, upd, in_scale, decay

    def direction(c, d):
        ydiag, upd, in_scale, decay = scan_local(c, d)
        hprev = h_ref[d]
        yoff = jnp.dot(cs_ref[c], hprev.astype(BF16), preferred_element_type=F32)
        acc_ref[c] += ydiag + yoff * in_scale
        h_ref[d] = hprev * decay + upd

    def scan_body(i, carry):
        direction(i, 0)
        direction(nc - 1 - i, 1)
        return carry

    lax.fori_loop(0, nc, scan_body, 0, unroll=4 if nc % 4 == 0 else 2)

    gn = gn_ref[...]

    def out_body(c, carry):
        r0 = pl.multiple_of(c * q, q)
        y = acc_ref[c] * _silu(z_ref[0, pl.ds(r0, q), :].astype(F32))
        o_ref[0, pl.ds(r0, q), :] = (_rms(y) * gn).astype(BF16)
        return carry

    lax.fori_loop(0, nc, out_body, 0, unroll=2)


def _ssd(main, dtt, cw, cb, dtb, alog, dsk, gn):
    b, s, _ = main.shape
    g = SSD_GROUPS
    n = SSD_STATE
    q = SSD_CHUNK
    nc = s // q
    gc = GW + 2 * n
    xbc_blk0 = (g * GW) // gc
    return pl.pallas_call(
        _ssd_kernel,
        out_shape=jax.ShapeDtypeStruct((b, s, g * GW), BF16),
        grid=(b, g),
        in_specs=[pl.BlockSpec((1, s, GW), lambda i, j: (i, 0, j)),
                  pl.BlockSpec((1, s, gc), lambda i, j: (i, 0, xbc_blk0 + j)),
                  pl.BlockSpec((1, nc, 2 * HPG, q), lambda i, j: (i, 0, j, 0)),
                  pl.BlockSpec((CONV_WIDTH, gc), lambda i, j: (0, j)),
                  pl.BlockSpec((1, gc), lambda i, j: (0, j)),
                  pl.BlockSpec((1, 2 * HPG, LANES), lambda i, j: (j, 0, 0)),
                  pl.BlockSpec((1, 2 * HPG, LANES), lambda i, j: (j, 0, 0)),
                  pl.BlockSpec((1, 1, GW), lambda i, j: (j, 0, 0)),
                  pl.BlockSpec((1, GW), lambda i, j: (0, j))],
        out_specs=pl.BlockSpec((1, s, GW), lambda i, j: (i, 0, j)),
        scratch_shapes=[pltpu.VMEM((nc, HPG // 2, 2 * q, GW), BF16),
                        pltpu.VMEM((nc, n, q), F32),
                        pltpu.VMEM((nc, q, n), BF16),
                        pltpu.VMEM((nc, q, q), F32),
                        pltpu.VMEM((nc, q, GW), F32),
                        pltpu.VMEM((nc, q, LANES), F32),
                        pltpu.VMEM((nc, 6 * HPG, q), F32),
                        pltpu.VMEM((2, n, GW), F32)],
        compiler_params=_cparams(2),
        name="ssd",
    )(main, main, dtt, cw, cb, dtb, alog, dsk, gn)


def _merge_kernel(yg_ref, at_ref, gt_ref, x_ref, g1_ref, wa_ref, wb_ref, wo_ref, o_ref):
    d = x_ref.shape[2]
    ya = jnp.dot(yg_ref[0], wa_ref[...], preferred_element_type=F32)
    attn = jnp.concatenate([at_ref[0, p] for p in range(at_ref.shape[1])], axis=1)
    yb = jnp.dot(attn, wb_ref[...], preferred_element_type=F32)
    gates = jax.nn.sigmoid(gt_ref[0].astype(F32))
    mixed = (gates[:, 0:d] * ya + gates[:, d:2 * d] * yb).astype(BF16)
    o_ref[0] = x_ref[0] + g1_ref[0] * jnp.dot(mixed, wo_ref[...], preferred_element_type=F32)


def _merge(yg, attn, main, x, gate1, wa, wb, wo, tm):
    b, s, d = x.shape
    gate_blk = (main.shape[2] - 2 * d) // (2 * d)
    return pl.pallas_call(
        _merge_kernel,
        out_shape=jax.ShapeDtypeStruct((b, s, d), F32),
        grid=(b, s // tm),
        in_specs=[pl.BlockSpec((1, tm, yg.shape[2]), lambda i, j: (i, j, 0)),
                  pl.BlockSpec((1, attn.shape[1], tm, attn.shape[3]), lambda i, j: (i, 0, j, 0)),
                  pl.BlockSpec((1, tm, 2 * d), lambda i, j: (i, j, gate_blk)),
                  pl.BlockSpec((1, tm, d), lambda i, j: (i, j, 0)),
                  pl.BlockSpec((1, 1, d), lambda i, j: (i, 0, 0)),
                  _resident(wa.shape), _resident(wb.shape), _resident(wo.shape)],
        out_specs=pl.BlockSpec((1, tm, d), lambda i, j: (i, j, 0)),
        compiler_params=_cparams(2),
        name="merge",
    )(yg, attn, main, x, gate1, wa, wb, wo)


def _mlp_kernel(x_ref, sc_ref, sh_ref, g2_ref, gn_ref, gf_ref, w1_ref, w2_ref, o_ref,
                *, ff_chunk):
    x = x_ref[0]
    h = _rms(x) * gn_ref[...]
    hb = (h * (1.0 + sc_ref[0]) + sh_ref[0]).astype(BF16)
    acc = None
    for j in range(w1_ref.shape[1] // ff_chunk):
        sl = slice(j * ff_chunk, (j + 1) * ff_chunk)
        u = jnp.maximum(jnp.dot(hb, w1_ref[:, sl], preferred_element_type=F32), 0.0)
        term = jnp.dot((u * u).astype(BF16), w2_ref[sl, :], preferred_element_type=F32)
        acc = term if acc is None else acc + term
    y = x + g2_ref[0] * acc
    o_ref[0] = _rms(y) * gf_ref[...]


def _mlp(x, scale, shift, gate2, gn, gf, w1, w2, tm):
    b, s, d = x.shape
    vec = pl.BlockSpec((1, 1, d), lambda i, j: (i, 0, 0))
    return pl.pallas_call(
        functools.partial(_mlp_kernel, ff_chunk=1024),
        out_shape=jax.ShapeDtypeStruct((b, s, d), F32),
        grid=(b, s // tm),
        in_specs=[pl.BlockSpec((1, tm, d), lambda i, j: (i, j, 0)), vec, vec, vec,
                  _resident(gn.shape), _resident(gf.shape),
                  _resident(w1.shape), _resident(w2.shape)],
        out_specs=pl.BlockSpec((1, tm, d), lambda i, j: (i, j, 0)),
        compiler_params=_cparams(2),
        name="mlp",
    )(x, scale, shift, gate2, gn, gf, w1, w2)


def _rope_tables(s):
    inv = 1.0 / (ROPE_THETA ** (jnp.arange(0, QK_ROPE, 2, dtype=F32) / QK_ROPE))
    ang = jnp.arange(s, dtype=F32)[:, None] * inv[None, :]
    cos, sin = jnp.cos(ang), jnp.sin(ang)
    zn = jnp.zeros((s, QK_NOPE), F32)
    zp = jnp.zeros((s, HEAD_PAD - QK_NOPE - QK_ROPE), F32)
    kcos = jnp.concatenate([zn, cos, cos, zp], axis=1)
    ksin = jnp.concatenate([zn, -sin, sin, zp], axis=1)
    scale = (QK_NOPE + QK_ROPE) ** -0.5 * LOG2E
    qcos = jnp.concatenate([jnp.ones((s, QK_NOPE), F32), cos, cos, zp], axis=1) * scale
    qsin = ksin * scale
    return qcos, qsin, kcos, ksin


def _group_xbc(a, d_inner):
    lead = a.shape[:-1]
    nbc = SSD_GROUPS * SSD_STATE
    x = a[..., :d_inner].reshape(lead + (SSD_GROUPS, GW))
    bm = a[..., d_inner:d_inner + nbc].reshape(lead + (SSD_GROUPS, SSD_STATE))
    cm = a[..., d_inner + nbc:].reshape(lead + (SSD_GROUPS, SSD_STATE))
    return jnp.concatenate([x, bm, cm], axis=-1).reshape(lead + (-1,))


def _prep_weights(w_in, w_q_b, w_kv_b, d_inner):
    d = w_in.shape[0]
    conv_ch = d_inner + 2 * SSD_GROUPS * SSD_STATE
    n_heads = d_inner // SSD_HEAD_DIM
    o_z, o_x = 0, d_inner
    o_dt = o_x + conv_ch
    o_q = o_dt + 2 * n_heads
    o_kv = o_q + Q_LORA
    o_g = o_kv + KV_LORA + QK_ROPE
    half = QK_ROPE // 2
    w_main = jnp.concatenate([w_in[:, o_z:o_x], _group_xbc(w_in[:, o_x:o_dt], d_inner),
                              w_in[:, o_g:]], axis=1).astype(BF16)
    kr = w_in[:, o_kv + KV_LORA:o_g]
    kr_sw = jnp.concatenate([kr[:, half:], kr[:, :half]], axis=1)
    zl = jnp.zeros((d, QK_NOPE), F32)
    zr = jnp.zeros((d, HEAD_PAD - QK_NOPE - QK_ROPE), F32)
    w_small = jnp.concatenate(
        [w_in[:, o_q:o_kv], w_in[:, o_kv:o_kv + KV_LORA], zl, kr, zr, zl, kr_sw, zr],
        axis=1).astype(BF16)
    w_dt = w_in[:, o_dt:o_q].T.reshape(2, SSD_GROUPS, HPG, d)
    w_dt = jnp.transpose(w_dt, (1, 0, 2, 3)).reshape(2 * n_heads, d).astype(BF16)

    lq = w_q_b.shape[0]
    wq3 = w_q_b.reshape(lq, MLA_HEADS, QK_NOPE + QK_ROPE)
    q_nope, q_r = wq3[..., :QK_NOPE], wq3[..., QK_NOPE:]
    zq = jnp.zeros((lq, MLA_HEADS, HEAD_PAD - QK_NOPE - QK_ROPE), F32)
    wq = jnp.concatenate([q_nope, q_r, zq], axis=-1).reshape(lq, -1).astype(BF16)
    lk = w_kv_b.shape[0]
    wkv3 = w_kv_b.reshape(lk, MLA_HEADS, QK_NOPE + V_HEAD)
    zk = jnp.zeros((lk, MLA_HEADS, HEAD_PAD - QK_NOPE), F32)
    wk = jnp.concatenate([wkv3[..., :QK_NOPE], zk], axis=-1).reshape(lk, -1).astype(BF16)
    v4 = wkv3[..., QK_NOPE:].reshape(lk, MLA_HEADS // 2, 2, V_HEAD)
    zv = jnp.zeros((lk, MLA_HEADS // 2, V_HEAD), F32)
    wv = jnp.stack([jnp.concatenate([v4[:, :, 0], zv], axis=-1),
                    jnp.concatenate([zv, v4[:, :, 1]], axis=-1)], axis=2)
    wv = wv.reshape(lk, MLA_HEADS * HEAD_PAD).astype(BF16)
    pair_ones = jnp.zeros((2 * HEAD_PAD,), F32).at[V_HEAD].set(1.0).at[HEAD_PAD].set(1.0)
    vone = jnp.tile(pair_ones, MLA_HEADS // 2).reshape(1, MLA_HEADS * HEAD_PAD)
    return w_main, w_small, w_dt, wq, wk, wv, vone


def _group_rows(v):
    f, bwd = v
    g = jnp.concatenate([f.reshape(SSD_GROUPS, HPG), bwd.reshape(SSD_GROUPS, HPG)], axis=1)
    return jnp.broadcast_to(g[:, :, None], (SSD_GROUPS, 2 * HPG, LANES)).astype(F32)


def kernel(x_prompt, x_sample, c_prompt, c_sample, w_ada, b_ada, g_norm1, w_in, conv_w,
           conv_b, dt_bias_fwd, dt_bias_bwd, a_log_fwd, a_log_bwd, d_skip, g_ssd_norm,
           w_ssd_out, g_q_norm, w_q_b, g_kv_norm, w_kv_b, w_mla_out, w_o, g_norm2,
           w_mlp_in, w_mlp_out, g_final):
    assert w_ada.shape[0] == 1, "single layer"
    d = x_prompt.shape[2]
    d_inner = w_ssd_out.shape[1]
    w_main, w_small, w_dt, wq, wk, wv, vone = _prep_weights(
        w_in[0], w_q_b[0], w_kv_b[0], d_inner)
    wa, wb, wo = (w_ssd_out[0].astype(BF16), w_mla_out[0].astype(BF16), w_o[0].astype(BF16))
    w1, w2 = w_mlp_in[0].astype(BF16), w_mlp_out[0].astype(BF16)
    row = lambda v: v.reshape(1, -1).astype(F32)
    dtb = _group_rows((dt_bias_fwd[0], dt_bias_bwd[0]))
    alog = _group_rows((a_log_fwd[0], a_log_bwd[0]))
    dsk = jnp.repeat(d_skip[0].astype(F32), SSD_HEAD_DIM).reshape(SSD_GROUPS, 1, GW)
    cw = _group_xbc(conv_w[0].astype(F32), d_inner)
    cbias = _group_xbc(row(conv_b[0]), d_inner)

    nb = c_prompt.shape[0]
    ada = _ada(jnp.concatenate([c_prompt, c_sample], axis=0), w_ada[0], b_ada[0])

    def trunk(x, ada_rows):
        b, s, _ = x.shape
        mods = [ada_rows[:, None, i * d:(i + 1) * d] for i in range(N_ADA)]
        shift1, scale1, gate1, shift2, scale2, gate2 = mods
        tm = min(512, s)
        main, small, dtt = _inproj(x, scale1, shift1, row(g_norm1[0]), w_main, w_small, w_dt, tm)
        q, k, v = _qkv(small, _rope_tables(s), row(g_q_norm[0]), wq,
                       row(g_kv_norm[0]), wk, wv, vone, tm)
        tq = min(s, max(SSD_CHUNK, ATTN_LOGIT_ELEMS // s))
        attn = _attn(q, k, v, tq, min(512, s))
        yg = _ssd(main, dtt, cw, cbias, dtb, alog, dsk, row(g_ssd_norm[0]))
        x1 = _merge(yg, attn, main, x, gate1, wa, wb, wo, tm)
        return _mlp(x1, scale2, shift2, gate2, row(g_norm2[0]), row(g_final), w1, w2, tm)

    return trunk(x_prompt, ada[:nb]), trunk(x_sample, ada[nb:])
```

```python
import functools
import math

import jax
import jax.numpy as jnp
from jax import lax
from jax.experimental import pallas as pl
from jax.experimental.pallas import tpu as pltpu

F32 = jnp.float32
BF16 = jnp.bfloat16

SSD_HEAD_DIM = 64
SSD_GROUPS = 8
SSD_STATE = 128
SSD_CHUNK = 128
CONV_WIDTH = 5
MLA_HEADS = 16
Q_LORA = 384
KV_LORA = 256
QK_NOPE = 64
QK_ROPE = 32
V_HEAD = 64
ROPE_THETA = 10000.0
N_ADA = 6
EPS = 1e-6
LOG2E = math.log2(math.e)

LANES = 128
HEAD_PAD = 128
VMEM_LIMIT = 56 * 1024 * 1024

HPG = 4
GW = HPG * SSD_HEAD_DIM
HALO = 16

CKV_OFF = Q_LORA
KR_OFF = CKV_OFF + KV_LORA
KRS_OFF = KR_OFF + LANES
SMALL_W = KRS_OFF + LANES


def _cparams(n_axes):
    return pltpu.CompilerParams(
        dimension_semantics=("arbitrary",) * n_axes, vmem_limit_bytes=VMEM_LIMIT)


def _resident(shape):
    nd = len(shape)
    return pl.BlockSpec(shape, lambda *_: (0,) * nd, pipeline_mode=pl.Buffered(1))


def _rms(x):
    return x * lax.rsqrt(jnp.mean(x * x, axis=-1, keepdims=True) + EPS)


def _silu(x):
    return x * jax.nn.sigmoid(x)


def _ada_kernel(c_ref, w_ref, b_ref, o_ref):
    s = _silu(c_ref[...])
    o_ref[...] = jnp.dot(s, w_ref[...], precision=lax.Precision.HIGHEST,
                         preferred_element_type=F32) + b_ref[...]


def _ada(c, w, b):
    n, d = c.shape
    dout = w.shape[1]
    tn = d
    return pl.pallas_call(
        _ada_kernel,
        out_shape=jax.ShapeDtypeStruct((n, dout), F32),
        grid=(dout // tn,),
        in_specs=[pl.BlockSpec((n, d), lambda j: (0, 0)),
                  pl.BlockSpec((d, tn), lambda j: (0, j)),
                  pl.BlockSpec((1, tn), lambda j: (0, j))],
        out_specs=pl.BlockSpec((n, tn), lambda j: (0, j)),
        compiler_params=_cparams(1),
        name="ada",
    )(c, w, b.reshape(1, dout))


def _inproj_kernel(x_ref, sc_ref, sh_ref, g_ref, wm_ref, ws_ref, wdt_ref,
                   main_ref, small_ref, dtt_ref, *, col_chunk):
    h = _rms(x_ref[0]) * g_ref[...]
    h = h * (1.0 + sc_ref[0]) + sh_ref[0]
    hb = h.astype(BF16)
    for j in range(wm_ref.shape[1] // col_chunk):
        sl = slice(j * col_chunk, (j + 1) * col_chunk)
        main_ref[0, :, sl] = jnp.dot(hb, wm_ref[:, sl], preferred_element_type=F32).astype(BF16)
    small_ref[0] = jnp.dot(hb, ws_ref[...], preferred_element_type=F32)
    dtt = lax.dot_general(wdt_ref[...], hb, (((1,), (1,)), ((), ())), preferred_element_type=F32)
    for c in range(dtt_ref.shape[1]):
        dtt_ref[0, c] = dtt[:, c * SSD_CHUNK:(c + 1) * SSD_CHUNK]


def _inproj(x, scale, shift, g, wm, ws, wdt, tm):
    b, s, d = x.shape
    nm, ns, nh = wm.shape[1], ws.shape[1], wdt.shape[0]
    return pl.pallas_call(
        functools.partial(_inproj_kernel, col_chunk=2048),
        out_shape=(jax.ShapeDtypeStruct((b, s, nm), BF16),
                   jax.ShapeDtypeStruct((b, s, ns), F32),
                   jax.ShapeDtypeStruct((b, s // SSD_CHUNK, nh, SSD_CHUNK), F32)),
        grid=(b, s // tm),
        in_specs=[pl.BlockSpec((1, tm, d), lambda i, j: (i, j, 0)),
                  pl.BlockSpec((1, 1, d), lambda i, j: (i, 0, 0)),
                  pl.BlockSpec((1, 1, d), lambda i, j: (i, 0, 0)),
                  _resident(g.shape), _resident(wm.shape), _resident(ws.shape),
                  _resident(wdt.shape)],
        out_specs=(pl.BlockSpec((1, tm, nm), lambda i, j: (i, j, 0)),
                   pl.BlockSpec((1, tm, ns), lambda i, j: (i, j, 0)),
                   pl.BlockSpec((1, tm // SSD_CHUNK, nh, SSD_CHUNK), lambda i, j: (i, j, 0, 0))),
        compiler_params=_cparams(2),
        name="in_proj",
    )(x, scale, shift, g, wm, ws, wdt)


def _qkv_kernel(sm_ref, qc_ref, qs_ref, kc_ref, ks_ref, gq_ref, wq_ref,
                gkv_ref, wk_ref, wv_ref, vone_ref, q_ref, k_ref, v_ref):
    sm = sm_ref[0]
    qn = (_rms(sm[:, 0:Q_LORA]) * gq_ref[...]).astype(BF16)
    cn = (_rms(sm[:, CKV_OFF:KR_OFF]) * gkv_ref[...]).astype(BF16)
    krope = sm[:, KR_OFF:KRS_OFF] * kc_ref[...] + sm[:, KRS_OFF:SMALL_W] * ks_ref[...]
    qcos, qsin = qc_ref[...], qs_ref[...]
    q1 = jnp.dot(qn, wq_ref[...], preferred_element_type=F32)
    k1 = jnp.dot(cn, wk_ref[...], preferred_element_type=F32)
    v = jnp.dot(cn, wv_ref[...], preferred_element_type=F32) + vone_ref[...]
    half = QK_ROPE // 2
    lane = lax.broadcasted_iota(jnp.int32, (sm.shape[0], HEAD_PAD), 1)
    first_half = lane < QK_NOPE + half
    for h in range(MLA_HEADS):
        sl = slice(h * HEAD_PAD, (h + 1) * HEAD_PAD)
        t = q1[:, sl]
        swapped = jnp.where(first_half, pltpu.roll(t, HEAD_PAD - half, axis=1),
                            pltpu.roll(t, half, axis=1))
        q_ref[0, h] = (t * qcos + swapped * qsin).astype(BF16)
        k_ref[0, h] = (k1[:, sl] + krope).astype(BF16)
        v_ref[0, h] = v[:, sl].astype(BF16)


def _qkv(small, tabs, gq, wq, gkv, wk, wv, vone, tm):
    b, s, ns = small.shape
    tab_spec = pl.BlockSpec((tm, LANES), lambda i, j: (j, 0))
    head_major = jax.ShapeDtypeStruct((b, MLA_HEADS, s, HEAD_PAD), BF16)
    out_spec = pl.BlockSpec((1, MLA_HEADS, tm, HEAD_PAD), lambda i, j: (i, 0, j, 0))
    return pl.pallas_call(
        _qkv_kernel,
        out_shape=(head_major, head_major, head_major),
        grid=(b, s // tm),
        in_specs=[pl.BlockSpec((1, tm, ns), lambda i, j: (i, j, 0)),
                  tab_spec, tab_spec, tab_spec, tab_spec,
                  _resident(gq.shape), _resident(wq.shape),
                  _resident(gkv.shape), _resident(wk.shape), _resident(wv.shape),
                  _resident(vone.shape)],
        out_specs=(out_spec, out_spec, out_spec),
        compiler_params=_cparams(2),
        name="qkv",
    )(small, *tabs, gq, wq, gkv, wk, wv, vone)


ATTN_HEADS_PER_STEP = 8
ATTN_LOGIT_ELEMS = 1 << 20


def _attn_kernel(q_ref, k_ref, v_ref, o_ref, s_ref, m_ref, *, tk):
    s = k_ref.shape[2]
    tq = q_ref.shape[2]
    npair = q_ref.shape[1] // 2
    nk = s // tk
    lane = lax.broadcasted_iota(jnp.int32, (tq, LANES), 1)

    def qk_pass(pp, slot):
        for e in range(2):
            q = q_ref[0, 2 * pp + e]
            mpart = jnp.full((tq, LANES), -jnp.inf, F32)
            for j in range(nk):
                sc = lax.dot_general(q, k_ref[0, 2 * pp + e, j * tk:(j + 1) * tk, :],
                                     (((1,), (1,)), ((), ())), preferred_element_type=F32)
                s_ref[slot, e, :, j * tk:(j + 1) * tk] = sc
                for c in range(tk // LANES):
                    mpart = jnp.maximum(mpart, sc[:, c * LANES:(c + 1) * LANES])
            m_ref[slot, e] = jnp.broadcast_to(jnp.max(mpart, axis=-1, keepdims=True), (tq, LANES))

    def pv_pass(pp, slot):
        out = None
        for e in range(2):
            mb = m_ref[slot, e]
            acc = jnp.zeros((tq, LANES), F32)
            for j in range(nk):
                cols = [jnp.exp2(s_ref[slot, e, :, j * tk + c * LANES:j * tk + (c + 1) * LANES]
                                 - mb).astype(BF16) for c in range(tk // LANES)]
                acc = acc + jnp.dot(jnp.concatenate(cols, axis=1),
                                    v_ref[0, 2 * pp + e, j * tk:(j + 1) * tk, :],
                                    preferred_element_type=F32)
            ones_lane = V_HEAD if e == 0 else 0
            is_value = (lane < V_HEAD) if e == 0 else (lane >= V_HEAD)
            o = jnp.where(is_value, acc / acc[:, ones_lane:ones_lane + 1], 0.0)
            out = o if out is None else out + o
        o_ref[0, pp] = out.astype(BF16)

    qk_pass(0, 0)

    def body(pp, carry):
        slot = pp % 2
        pv_pass(pp, slot)
        qk_pass(pp + 1, 1 - slot)
        return carry

    lax.fori_loop(0, npair - 1, body, 0)
    pv_pass(npair - 1, (npair - 1) % 2)


def _attn(q, k, v, tq, tk):
    b, nh, s, _ = q.shape
    hg = ATTN_HEADS_PER_STEP
    kv_spec = pl.BlockSpec((1, hg, s, HEAD_PAD), lambda i, h, j: (i, h, 0, 0))
    return pl.pallas_call(
        functools.partial(_attn_kernel, tk=tk),
        out_shape=jax.ShapeDtypeStruct((b, nh // 2, s, 2 * V_HEAD), BF16),
        grid=(b, nh // hg, s // tq),
        in_specs=[pl.BlockSpec((1, hg, tq, HEAD_PAD), lambda i, h, j: (i, h, j, 0)),
                  kv_spec, kv_spec],
        out_specs=pl.BlockSpec((1, hg // 2, tq, 2 * V_HEAD), lambda i, h, j: (i, h, j, 0)),
        scratch_shapes=[pltpu.VMEM((2, 2, tq, s), F32), pltpu.VMEM((2, 2, tq, LANES), F32)],
        compiler_params=_cparams(3),
        name="attn",
    )(q, k, v)


def _pair_expand(cols, low_half):
    return jnp.concatenate([jnp.where(low_half, cols[0], cols[1]),
                            jnp.where(low_half, cols[2], cols[3])], axis=1)


def _ssd_kernel(z_ref, xbc_ref, dtt_ref, cw_ref, cbias_ref, dtb_ref, alog_ref, dsk_ref, gn_ref,
                o_ref, xp_ref, bt_ref, cs_ref, cb_ref, acc_ref, col_ref, row_ref, h_ref):
    s = xbc_ref.shape[1]
    q = SSD_CHUNK
    n = SSD_STATE
    nc = s // q
    nh2 = 2 * HPG

    a2 = -jnp.exp(alog_ref[0]) * LOG2E
    dt = jax.nn.softplus(dtt_ref[0] + dtb_ref[0][None]).reshape(nc * nh2, q)
    pre = suf = (dt.reshape(nc, nh2, q) * a2[None]).reshape(nc * nh2, q)
    lane = lax.broadcasted_iota(jnp.int32, (nc * nh2, q), 1)
    is_fwd = (lax.broadcasted_iota(jnp.int32, (nc * nh2, q), 0) % nh2) < HPG
    k = 1
    while k < q:
        pre = pre + jnp.where(lane >= k, pltpu.roll(pre, k, axis=1), 0.0)
        suf = suf + jnp.where(lane < q - k, pltpu.roll(suf, q - k, axis=1), 0.0)
        k *= 2
    cum = jnp.where(is_fwd, pre, suf)
    end = jnp.where(is_fwd, jnp.broadcast_to(cum[:, q - 1:q], cum.shape),
                    jnp.broadcast_to(cum[:, 0:1], cum.shape))
    row_ref[:, 0:nh2, :] = (cum - jnp.log2(dt)).reshape(nc, nh2, q)
    row_ref[:, nh2:2 * nh2, :] = (dt * jnp.exp2(end - cum)).reshape(nc, nh2, q)
    row_ref[:, 2 * nh2:3 * nh2, :] = jnp.exp2(end).reshape(nc, nh2, q)
    row_ref[:, 3 * nh2:4 * nh2, :] = cum.reshape(nc, nh2, q)
    zeros_pad = jnp.zeros((q - nh2, q), F32)

    w = cw_ref[...]
    bias = cbias_ref[...]
    dsk = dsk_ref[0]
    pad = CONV_WIDTH // 2

    win = 2 * q
    sh_r = lax.broadcasted_iota(jnp.int32, (q, win), 0)
    sh_c = lax.broadcasted_iota(jnp.int32, (q, win), 1)
    taps = [t for t in range(CONV_WIDTH) if t != pad]
    shifts = jnp.concatenate([jnp.where(sh_c == sh_r + (HALO - pad + t), 1.0, 0.0).astype(BF16)
                              for t in taps], axis=0)

    zero_halo = jnp.zeros((HALO, GW + 2 * n), BF16)
    tail = jnp.zeros((win - q - 2 * HALO, GW + 2 * n), BF16)

    col_halves = (slice(0, GW), slice(GW, 2 * GW))
    lane_head = lax.broadcasted_iota(jnp.int32, (q, GW), 1) // SSD_HEAD_DIM

    def conv_shift(c):
        r0 = pl.multiple_of(c * q, q)
        main = xbc_ref[0, pl.ds(r0, q), :]
        p0 = pl.multiple_of(jnp.maximum(r0 - HALO, 0), HALO)
        n0 = pl.multiple_of(jnp.minimum(r0 + q, s - HALO), HALO)
        prev = jnp.where(c > 0, xbc_ref[0, pl.ds(p0, HALO), :], zero_halo)
        nxt = jnp.where(c < nc - 1, xbc_ref[0, pl.ds(n0, HALO), :], zero_halo)
        window = jnp.concatenate([prev, main, nxt, tail], axis=0)
        return main, [jnp.dot(shifts, window[:, sl], preferred_element_type=F32)
                      for sl in col_halves]

    def conv_finish(c, main, shifted):
        halves = []
        for sl, sh in zip(col_halves, shifted):
            out = bias[:, sl] + main[:, sl].astype(F32) * w[pad:pad + 1, sl]
            for i, t in enumerate(taps):
                out = out + sh[i * q:(i + 1) * q] * w[t:t + 1, sl]
            halves.append(_silu(out))
        xc = halves[0]
        bconv = halves[1][:, 0:n]
        cmat = halves[1][:, n:2 * n].astype(BF16)
        xb = xc.astype(BF16)
        zero_b = jnp.zeros_like(xb)
        for jp in range(HPG // 2):
            xp_ref[c, jp] = jnp.concatenate(
                [jnp.where(lane_head == j, xb, zero_b) for j in (2 * jp, 2 * jp + 1)], axis=0)
        acc_ref[c] = xc * dsk
        cs_ref[c] = cmat
        col_ref[c] = jnp.concatenate([row_ref[c, 3 * nh2:4 * nh2, :], zeros_pad], axis=0).T
        bt_ref[c] = bconv.T
        cb_ref[c] = lax.dot_general(cmat, bconv.astype(BF16), (((1,), (1,)), ((), ())),
                                    preferred_element_type=F32)

    conv_group = 8 if nc % 8 == 0 else 2

    def conv_body(i, carry):
        cs = [i * conv_group + u for u in range(conv_group)]
        staged = [conv_shift(c) for c in cs]
        for c, (main, shifted) in zip(cs, staged):
            conv_finish(c, main, shifted)
        return carry

    lax.fori_loop(0, nc // conv_group, conv_body, 0)

    h_ref[...] = jnp.zeros_like(h_ref)
    row_i = lax.broadcasted_iota(jnp.int32, (q, q), 0)
    col_i = lax.broadcasted_iota(jnp.int32, (q, q), 1)
    low_half = col_i < SSD_HEAD_DIM
    low_row = low_half[0:1, :]
    masks = (row_i >= col_i, row_i <= col_i)

    def scan_local(c, d):
        colblk = col_ref[c]
        cb = cb_ref[c]
        bt = bt_ref[c]
        row = lambda kind, j: row_ref[c, kind * nh2 + d * HPG + j:kind * nh2 + d * HPG + j + 1, :]
        cum_b = [jnp.broadcast_to(colblk[:, d * HPG + j:d * HPG + j + 1], (q, LANES))
                 for j in range(HPG)]
        ydiag = upd = None
        for jp in range(HPG // 2):
            ms, bts = [], []
            for j in (2 * jp, 2 * jp + 1):
                dec = jnp.exp2(jnp.where(masks[d], cum_b[j] - row(0, j), -jnp.inf))
                ms.append((cb * dec).astype(BF16))
                bts.append((bt * row(1, j)).astype(BF16))
            lhs = jnp.concatenate([jnp.concatenate(ms, axis=1), jnp.concatenate(bts, axis=1)],
                                  axis=0)
            both = jnp.dot(lhs, xp_ref[c, jp], preferred_element_type=F32)
            ydiag = both[0:q] if ydiag is None else ydiag + both[0:q]
            upd = both[q:q + n] if upd is None else upd + both[q:q + n]
        in_scale = jnp.exp2(_pair_expand(cum_b, low_half))
        decay = _pair_expand([row(2, j) for j in range(HPG)], low_row)
        return ydiag, upd, in_scale, decay

    def direction(c, d):
        ydiag, upd, in_scale, decay = scan_local(c, d)
        hprev = h_ref[d]
        yoff = jnp.dot(cs_ref[c], hprev.astype(BF16), preferred_element_type=F32)
        acc_ref[c] += ydiag + yoff * in_scale
        h_ref[d] = hprev * decay + upd

    def scan_body(i, carry):
        direction(i, 0)
        direction(nc - 1 - i, 1)
        return carry

    lax.fori_loop(0, nc, scan_body, 0, unroll=8 if nc % 8 == 0 else 2)

    gn = gn_ref[...]

    def out_body(c, carry):
        r0 = pl.multiple_of(c * q, q)
        y = acc_ref[c] * _silu(z_ref[0, pl.ds(r0, q), :].astype(F32))
        o_ref[0, pl.ds(r0, q), :] = (_rms(y) * gn).astype(BF16)
        return carry

    lax.fori_loop(0, nc, out_body, 0, unroll=4 if nc % 4 == 0 else 2)


def _ssd(main, dtt, cw, cb, dtb, alog, dsk, gn):
    b, s, _ = main.shape
    g = SSD_GROUPS
    n = SSD_STATE
    q = SSD_CHUNK
    nc = s // q
    gc = GW + 2 * n
    xbc_blk0 = (g * GW) // gc
    return pl.pallas_call(
        _ssd_kernel,
        out_shape=jax.ShapeDtypeStruct((b, s, g * GW), BF16),
        grid=(b, g),
        in_specs=[pl.BlockSpec((1, s, GW), lambda i, j: (i, 0, j)),
                  pl.BlockSpec((1, s, gc), lambda i, j: (i, 0, xbc_blk0 + j)),
                  pl.BlockSpec((1, nc, 2 * HPG, q), lambda i, j: (i, 0, j, 0)),
                  pl.BlockSpec((CONV_WIDTH, gc), lambda i, j: (0, j)),
                  pl.BlockSpec((1, gc), lambda i, j: (0, j)),
                  pl.BlockSpec((1, 2 * HPG, LANES), lambda i, j: (j, 0, 0)),
                  pl.BlockSpec((1, 2 * HPG, LANES), lambda i, j: (j, 0, 0)),
                  pl.BlockSpec((1, 1, GW), lambda i, j: (j, 0, 0)),
                  pl.BlockSpec((1, GW), lambda i, j: (0, j))],
        out_specs=pl.BlockSpec((1, s, GW), lambda i, j: (i, 0, j)),
        scratch_shapes=[pltpu.VMEM((nc, HPG // 2, 2 * q, GW), BF16),
                        pltpu.VMEM((nc, n, q), F32),
                        pltpu.VMEM((nc, q, n), BF16),
                        pltpu.VMEM((nc, q, q), F32),
                        pltpu.VMEM((nc, q, GW), F32),
                        pltpu.VMEM((nc, q, LANES), F32),
                        pltpu.VMEM((nc, 8 * HPG, q), F32),
                        pltpu.VMEM((2, n, GW), F32)],
        compiler_params=_cparams(2),
        name="ssd",
    )(main, main, dtt, cw, cb, dtb, alog, dsk, gn)


def _merge_kernel(yg_ref, at_ref, gt_ref, x_ref, g1_ref, wa_ref, wb_ref, wo_ref, o_ref):
    d = x_ref.shape[2]
    ya = jnp.dot(yg_ref[0], wa_ref[...], preferred_element_type=F32)
    attn = jnp.concatenate([at_ref[0, p] for p in range(at_ref.shape[1])], axis=1)
    yb = jnp.dot(attn, wb_ref[...], preferred_element_type=F32)
    gates = jax.nn.sigmoid(gt_ref[0].astype(F32))
    mixed = (gates[:, 0:d] * ya + gates[:, d:2 * d] * yb).astype(BF16)
    o_ref[0] = x_ref[0] + g1_ref[0] * jnp.dot(mixed, wo_ref[...], preferred_element_type=F32)


def _merge(yg, attn, main, x, gate1, wa, wb, wo, tm):
    b, s, d = x.shape
    gate_blk = (main.shape[2] - 2 * d) // (2 * d)
    return pl.pallas_call(
        _merge_kernel,
        out_shape=jax.ShapeDtypeStruct((b, s, d), F32),
        grid=(b, s // tm),
        in_specs=[pl.BlockSpec((1, tm, yg.shape[2]), lambda i, j: (i, j, 0)),
                  pl.BlockSpec((1, attn.shape[1], tm, attn.shape[3]), lambda i, j: (i, 0, j, 0)),
                  pl.BlockSpec((1, tm, 2 * d), lambda i, j: (i, j, gate_blk)),
                  pl.BlockSpec((1, tm, d), lambda i, j: (i, j, 0)),
                  pl.BlockSpec((1, 1, d), lambda i, j: (i, 0, 0)),
                  _resident(wa.shape), _resident(wb.shape), _resident(wo.shape)],
        out_specs=pl.BlockSpec((1, tm, d), lambda i, j: (i, j, 0)),
        compiler_params=_cparams(2),
        name="merge",
    )(yg, attn, main, x, gate1, wa, wb, wo)


def _mlp_kernel(x_ref, sc_ref, sh_ref, g2_ref, gn_ref, gf_ref, w1_ref, w2_ref, o_ref,
                *, ff_chunk):
    x = x_ref[0]
    h = _rms(x) * gn_ref[...]
    hb = (h * (1.0 + sc_ref[0]) + sh_ref[0]).astype(BF16)
    acc = None
    for j in range(w1_ref.shape[1] // ff_chunk):
        sl = slice(j * ff_chunk, (j + 1) * ff_chunk)
        u = jnp.maximum(jnp.dot(hb, w1_ref[:, sl], preferred_element_type=F32), 0.0)
        term = jnp.dot((u * u).astype(BF16), w2_ref[sl, :], preferred_element_type=F32)
        acc = term if acc is None else acc + term
    y = x + g2_ref[0] * acc
    o_ref[0] = _rms(y) * gf_ref[...]


def _mlp(x, scale, shift, gate2, gn, gf, w1, w2, tm):
    b, s, d = x.shape
    vec = pl.BlockSpec((1, 1, d), lambda i, j: (i, 0, 0))
    return pl.pallas_call(
        functools.partial(_mlp_kernel, ff_chunk=1024),
        out_shape=jax.ShapeDtypeStruct((b, s, d), F32),
        grid=(b, s // tm),
        in_specs=[pl.BlockSpec((1, tm, d), lambda i, j: (i, j, 0)), vec, vec, vec,
                  _resident(gn.shape), _resident(gf.shape),
                  _resident(w1.shape), _resident(w2.shape)],
        out_specs=pl.BlockSpec((1, tm, d), lambda i, j: (i, j, 0)),
        compiler_params=_cparams(2),
        name="mlp",
    )(x, scale, shift, gate2, gn, gf, w1, w2)


def _rope_tables(s):
    inv = 1.0 / (ROPE_THETA ** (jnp.arange(0, QK_ROPE, 2, dtype=F32) / QK_ROPE))
    ang = jnp.arange(s, dtype=F32)[:, None] * inv[None, :]
    cos, sin = jnp.cos(ang), jnp.sin(ang)
    zn = jnp.zeros((s, QK_NOPE), F32)
    zp = jnp.zeros((s, HEAD_PAD - QK_NOPE - QK_ROPE), F32)
    kcos = jnp.concatenate([zn, cos, cos, zp], axis=1)
    ksin = jnp.concatenate([zn, -sin, sin, zp], axis=1)
    scale = (QK_NOPE + QK_ROPE) ** -0.5 * LOG2E
    qcos = jnp.concatenate([jnp.ones((s, QK_NOPE), F32), cos, cos, zp], axis=1) * scale
    qsin = ksin * scale
    return qcos, qsin, kcos, ksin


def _group_xbc(a, d_inner):
    lead = a.shape[:-1]
    nbc = SSD_GROUPS * SSD_STATE
    x = a[..., :d_inner].reshape(lead + (SSD_GROUPS, GW))
    bm = a[..., d_inner:d_inner + nbc].reshape(lead + (SSD_GROUPS, SSD_STATE))
    cm = a[..., d_inner + nbc:].reshape(lead + (SSD_GROUPS, SSD_STATE))
    return jnp.concatenate([x, bm, cm], axis=-1).reshape(lead + (-1,))


def _prep_weights(w_in, w_q_b, w_kv_b, d_inner):
    d = w_in.shape[0]
    conv_ch = d_inner + 2 * SSD_GROUPS * SSD_STATE
    n_heads = d_inner // SSD_HEAD_DIM
    o_z, o_x = 0, d_inner
    o_dt = o_x + conv_ch
    o_q = o_dt + 2 * n_heads
    o_kv = o_q + Q_LORA
    o_g = o_kv + KV_LORA + QK_ROPE
    half = QK_ROPE // 2
    w_main = jnp.concatenate([w_in[:, o_z:o_x], _group_xbc(w_in[:, o_x:o_dt], d_inner),
                              w_in[:, o_g:]], axis=1).astype(BF16)
    kr = w_in[:, o_kv + KV_LORA:o_g]
    kr_sw = jnp.concatenate([kr[:, half:], kr[:, :half]], axis=1)
    zl = jnp.zeros((d, QK_NOPE), F32)
    zr = jnp.zeros((d, HEAD_PAD - QK_NOPE - QK_ROPE), F32)
    w_small = jnp.concatenate(
        [w_in[:, o_q:o_kv], w_in[:, o_kv:o_kv + KV_LORA], zl, kr, zr, zl, kr_sw, zr],
        axis=1).astype(BF16)
    w_dt = w_in[:, o_dt:o_q].T.reshape(2, SSD_GROUPS, HPG, d)
    w_dt = jnp.transpose(w_dt, (1, 0, 2, 3)).reshape(2 * n_heads, d).astype(BF16)

    lq = w_q_b.shape[0]
    wq3 = w_q_b.reshape(lq, MLA_HEADS, QK_NOPE + QK_ROPE)
    q_nope, q_r = wq3[..., :QK_NOPE], wq3[..., QK_NOPE:]
    zq = jnp.zeros((lq, MLA_HEADS, HEAD_PAD - QK_NOPE - QK_ROPE), F32)
    wq = jnp.concatenate([q_nope, q_r, zq], axis=-1).reshape(lq, -1).astype(BF16)
    lk = w_kv_b.shape[0]
    wkv3 = w_kv_b.reshape(lk, MLA_HEADS, QK_NOPE + V_HEAD)
    zk = jnp.zeros((lk, MLA_HEADS, HEAD_PAD - QK_NOPE), F32)
    wk = jnp.concatenate([wkv3[..., :QK_NOPE], zk], axis=-1).reshape(lk, -1).astype(BF16)
    v4 = wkv3[..., QK_NOPE:].reshape(lk, MLA_HEADS // 2, 2, V_HEAD)
    zv = jnp.zeros((lk, MLA_HEADS // 2, V_HEAD), F32)
    wv = jnp.stack([jnp.concatenate([v4[:, :, 0], zv], axis=-1),
                    jnp.concatenate([zv, v4[:, :, 1]], axis=-1)], axis=2)
    wv = wv.reshape(lk, MLA_HEADS * HEAD_PAD).astype(BF16)
    pair_ones = jnp.zeros((2 * HEAD_PAD,), F32).at[V_HEAD].set(1.0).at[HEAD_PAD].set(1.0)
    vone = jnp.tile(pair_ones, MLA_HEADS // 2).reshape(1, MLA_HEADS * HEAD_PAD)
    return w_main, w_small, w_dt, wq, wk, wv, vone


def _group_rows(v):
    f, bwd = v
    g = jnp.concatenate([f.reshape(SSD_GROUPS, HPG), bwd.reshape(SSD_GROUPS, HPG)], axis=1)
    return jnp.broadcast_to(g[:, :, None], (SSD_GROUPS, 2 * HPG, LANES)).astype(F32)


def kernel(x_prompt, x_sample, c_prompt, c_sample, w_ada, b_ada, g_norm1, w_in, conv_w,
           conv_b, dt_bias_fwd, dt_bias_bwd, a_log_fwd, a_log_bwd, d_skip, g_ssd_norm,
           w_ssd_out, g_q_norm, w_q_b, g_kv_norm, w_kv_b, w_mla_out, w_o, g_norm2,
           w_mlp_in, w_mlp_out, g_final):
    assert w_ada.shape[0] == 1, "single layer"
    d = x_prompt.shape[2]
    d_inner = w_ssd_out.shape[1]
    w_main, w_small, w_dt, wq, wk, wv, vone = _prep_weights(
        w_in[0], w_q_b[0], w_kv_b[0], d_inner)
    wa, wb, wo = (w_ssd_out[0].astype(BF16), w_mla_out[0].astype(BF16), w_o[0].astype(BF16))
    w1, w2 = w_mlp_in[0].astype(BF16), w_mlp_out[0].astype(BF16)
    row = lambda v: v.reshape(1, -1).astype(F32)
    dtb = _group_rows((dt_bias_fwd[0], dt_bias_bwd[0]))
    alog = _group_rows((a_log_fwd[0], a_log_bwd[0]))
    dsk = jnp.repeat(d_skip[0].astype(F32), SSD_HEAD_DIM).reshape(SSD_GROUPS, 1, GW)
    cw = _group_xbc(conv_w[0].astype(F32), d_inner)
    cbias = _group_xbc(row(conv_b[0]), d_inner)

    nb = c_prompt.shape[0]
    ada = _ada(jnp.concatenate([c_prompt, c_sample], axis=0), w_ada[0], b_ada[0])

    def trunk(x, ada_rows):
        b, s, _ = x.shape
        mods = [ada_rows[:, None, i * d:(i + 1) * d] for i in range(N_ADA)]
        shift1, scale1, gate1, shift2, scale2, gate2 = mods
        tm = min(512, s)
        main, small, dtt = _inproj(x, scale1, shift1, row(g_norm1[0]), w_main, w_small, w_dt, tm)
        q, k, v = _qkv(small, _rope_tables(s), row(g_q_norm[0]), wq,
                       row(g_kv_norm[0]), wk, wv, vone, tm)
        tq = min(s, max(SSD_CHUNK, ATTN_LOGIT_ELEMS // s))
        attn = _attn(q, k, v, tq, min(512, s))
        yg = _ssd(main, dtt, cw, cbias, dtb, alog, dsk, row(g_ssd_norm[0]))
        x1 = _merge(yg, attn, main, x, gate1, wa, wb, wo, tm)
        return _mlp(x1, scale2, shift2, gate2, row(g_norm2[0]), row(g_final), w1, w2, tm)

    return trunk(x_prompt, ada[:nb]), trunk(x_sample, ada[nb:])
```

```python
import functools
import math

import jax
import jax.numpy as jnp
from jax import lax
from jax.experimental import pallas as pl
from jax.experimental.pallas import tpu as pltpu

F32 = jnp.float32
BF16 = jnp.bfloat16

SSD_HEAD_DIM = 64
SSD_GROUPS = 8
SSD_STATE = 128
SSD_CHUNK = 128
CONV_WIDTH = 5
MLA_HEADS = 16
Q_LORA = 384
KV_LORA = 256
QK_NOPE = 64
QK_ROPE = 32
V_HEAD = 64
ROPE_THETA = 10000.0
N_ADA = 6
EPS = 1e-6
LOG2E = math.log2(math.e)

LANES = 128
HEAD_PAD = 128
VMEM_LIMIT = 56 * 1024 * 1024

HPG = 4
GW = HPG * SSD_HEAD_DIM
HALO = 16

CKV_OFF = Q_LORA
KR_OFF = CKV_OFF + KV_LORA
KRS_OFF = KR_OFF + LANES
SMALL_W = KRS_OFF + LANES


def _cparams(n_axes):
    return pltpu.CompilerParams(
        dimension_semantics=("arbitrary",) * n_axes, vmem_limit_bytes=VMEM_LIMIT)


def _resident(shape):
    nd = len(shape)
    return pl.BlockSpec(shape, lambda *_: (0,) * nd, pipeline_mode=pl.Buffered(1))


def _rms(x):
    return x * lax.rsqrt(jnp.mean(x * x, axis=-1, keepdims=True) + EPS)


def _silu(x):
    return x * jax.nn.sigmoid(x)


def _ada_kernel(c_ref, w_ref, b_ref, o_ref):
    s = _silu(c_ref[...])
    o_ref[...] = jnp.dot(s, w_ref[...], precision=lax.Precision.HIGHEST,
                         preferred_element_type=F32) + b_ref[...]


def _ada(c, w, b):
    n, d = c.shape
    dout = w.shape[1]
    tn = d
    return pl.pallas_call(
        _ada_kernel,
        out_shape=jax.ShapeDtypeStruct((n, dout), F32),
        grid=(dout // tn,),
        in_specs=[pl.BlockSpec((n, d), lambda j: (0, 0)),
                  pl.BlockSpec((d, tn), lambda j: (0, j)),
                  pl.BlockSpec((1, tn), lambda j: (0, j))],
        out_specs=pl.BlockSpec((n, tn), lambda j: (0, j)),
        compiler_params=_cparams(1),
        name="ada",
    )(c, w, b.reshape(1, dout))


def _inproj_kernel(x_ref, sc_ref, sh_ref, g_ref, wm_ref, ws_ref, wdt_ref,
                   main_ref, small_ref, dtt_ref, *, col_chunk):
    h = _rms(x_ref[0]) * g_ref[...]
    h = h * (1.0 + sc_ref[0]) + sh_ref[0]
    hb = h.astype(BF16)
    for j in range(wm_ref.shape[1] // col_chunk):
        sl = slice(j * col_chunk, (j + 1) * col_chunk)
        main_ref[0, :, sl] = jnp.dot(hb, wm_ref[:, sl], preferred_element_type=F32).astype(BF16)
    small_ref[0] = jnp.dot(hb, ws_ref[...], preferred_element_type=F32)
    dtt = lax.dot_general(wdt_ref[...], hb, (((1,), (1,)), ((), ())), preferred_element_type=F32)
    for c in range(dtt_ref.shape[1]):
        dtt_ref[0, c] = dtt[:, c * SSD_CHUNK:(c + 1) * SSD_CHUNK]


def _inproj(x, scale, shift, g, wm, ws, wdt, tm):
    b, s, d = x.shape
    nm, ns, nh = wm.shape[1], ws.shape[1], wdt.shape[0]
    return pl.pallas_call(
        functools.partial(_inproj_kernel, col_chunk=2048),
        out_shape=(jax.ShapeDtypeStruct((b, s, nm), BF16),
                   jax.ShapeDtypeStruct((b, s, ns), F32),
                   jax.ShapeDtypeStruct((b, s // SSD_CHUNK, nh, SSD_CHUNK), F32)),
        grid=(b, s // tm),
        in_specs=[pl.BlockSpec((1, tm, d), lambda i, j: (i, j, 0)),
                  pl.BlockSpec((1, 1, d), lambda i, j: (i, 0, 0)),
                  pl.BlockSpec((1, 1, d), lambda i, j: (i, 0, 0)),
                  _resident(g.shape), _resident(wm.shape), _resident(ws.shape),
                  _resident(wdt.shape)],
        out_specs=(pl.BlockSpec((1, tm, nm), lambda i, j: (i, j, 0)),
                   pl.BlockSpec((1, tm, ns), lambda i, j: (i, j, 0)),
                   pl.BlockSpec((1, tm // SSD_CHUNK, nh, SSD_CHUNK), lambda i, j: (i, j, 0, 0))),
        compiler_params=_cparams(2),
        name="in_proj",
    )(x, scale, shift, g, wm, ws, wdt)


def _qkv_kernel(sm_ref, qc_ref, qs_ref, kc_ref, ks_ref, gq_ref, wq_ref,
                gkv_ref, wk_ref, wv_ref, vone_ref, q_ref, k_ref, v_ref):
    sm = sm_ref[0]
    qn = (_rms(sm[:, 0:Q_LORA]) * gq_ref[...]).astype(BF16)
    cn = (_rms(sm[:, CKV_OFF:KR_OFF]) * gkv_ref[...]).astype(BF16)
    krope = sm[:, KR_OFF:KRS_OFF] * kc_ref[...] + sm[:, KRS_OFF:SMALL_W] * ks_ref[...]
    qcos, qsin = qc_ref[...], qs_ref[...]
    q1 = jnp.dot(qn, wq_ref[...], preferred_element_type=F32)
    k1 = jnp.dot(cn, wk_ref[...], preferred_element_type=F32)
    v = jnp.dot(cn, wv_ref[...], preferred_element_type=F32) + vone_ref[...]
    half = QK_ROPE // 2
    lane = lax.broadcasted_iota(jnp.int32, (sm.shape[0], HEAD_PAD), 1)
    first_half = lane < QK_NOPE + half
    for h in range(MLA_HEADS):
        sl = slice(h * HEAD_PAD, (h + 1) * HEAD_PAD)
        t = q1[:, sl]
        swapped = jnp.where(first_half, pltpu.roll(t, HEAD_PAD - half, axis=1),
                            pltpu.roll(t, half, axis=1))
        q_ref[0, h] = (t * qcos + swapped * qsin).astype(BF16)
        k_ref[0, h] = (k1[:, sl] + krope).astype(BF16)
        v_ref[0, h] = v[:, sl].astype(BF16)


def _qkv(small, tabs, gq, wq, gkv, wk, wv, vone, tm):
    b, s, ns = small.shape
    tab_spec = pl.BlockSpec((tm, LANES), lambda i, j: (j, 0))
    head_major = jax.ShapeDtypeStruct((b, MLA_HEADS, s, HEAD_PAD), BF16)
    out_spec = pl.BlockSpec((1, MLA_HEADS, tm, HEAD_PAD), lambda i, j: (i, 0, j, 0))
    return pl.pallas_call(
        _qkv_kernel,
        out_shape=(head_major, head_major, head_major),
        grid=(b, s // tm),
        in_specs=[pl.BlockSpec((1, tm, ns), lambda i, j: (i, j, 0)),
                  tab_spec, tab_spec, tab_spec, tab_spec,
                  _resident(gq.shape), _resident(wq.shape),
                  _resident(gkv.shape), _resident(wk.shape), _resident(wv.shape),
                  _resident(vone.shape)],
        out_specs=(out_spec, out_spec, out_spec),
        compiler_params=_cparams(2),
        name="qkv",
    )(small, *tabs, gq, wq, gkv, wk, wv, vone)


ATTN_HEADS_PER_STEP = 8
ATTN_LOGIT_ELEMS = 1 << 20


def _attn_kernel(q_ref, k_ref, v_ref, o_ref, s_ref, m_ref, *, tk):
    s = k_ref.shape[2]
    tq = q_ref.shape[2]
    npair = q_ref.shape[1] // 2
    nk = s // tk
    lane = lax.broadcasted_iota(jnp.int32, (tq, LANES), 1)

    def qk_pass(pp, slot):
        for e in range(2):
            q = q_ref[0, 2 * pp + e]
            mpart = jnp.full((tq, LANES), -jnp.inf, F32)
            for j in range(nk):
                sc = lax.dot_general(q, k_ref[0, 2 * pp + e, j * tk:(j + 1) * tk, :],
                                     (((1,), (1,)), ((), ())), preferred_element_type=F32)
                s_ref[slot, e, :, j * tk:(j + 1) * tk] = sc
                for c in range(tk // LANES):
                    mpart = jnp.maximum(mpart, sc[:, c * LANES:(c + 1) * LANES])
            m_ref[slot, e] = jnp.broadcast_to(jnp.max(mpart, axis=-1, keepdims=True), (tq, LANES))

    def pv_pass(pp, slot):
        out = None
        for e in range(2):
            mb = m_ref[slot, e]
            acc = jnp.zeros((tq, LANES), F32)
            for j in range(nk):
                cols = [jnp.exp2(s_ref[slot, e, :, j * tk + c * LANES:j * tk + (c + 1) * LANES]
                                 - mb).astype(BF16) for c in range(tk // LANES)]
                acc = acc + jnp.dot(jnp.concatenate(cols, axis=1),
                                    v_ref[0, 2 * pp + e, j * tk:(j + 1) * tk, :],
                                    preferred_element_type=F32)
            ones_lane = V_HEAD if e == 0 else 0
            is_value = (lane < V_HEAD) if e == 0 else (lane >= V_HEAD)
            o = jnp.where(is_value, acc / acc[:, ones_lane:ones_lane + 1], 0.0)
            out = o if out is None else out + o
        o_ref[0, pp] = out.astype(BF16)

    qk_pass(0, 0)

    def body(pp, carry):
        slot = pp % 2
        pv_pass(pp, slot)
        qk_pass(pp + 1, 1 - slot)
        return carry

    lax.fori_loop(0, npair - 1, body, 0)
    pv_pass(npair - 1, (npair - 1) % 2)


def _attn(q, k, v, tq, tk):
    b, nh, s, _ = q.shape
    hg = ATTN_HEADS_PER_STEP
    kv_spec = pl.BlockSpec((1, hg, s, HEAD_PAD), lambda i, h, j: (i, h, 0, 0))
    return pl.pallas_call(
        functools.partial(_attn_kernel, tk=tk),
        out_shape=jax.ShapeDtypeStruct((b, nh // 2, s, 2 * V_HEAD), BF16),
        grid=(b, nh // hg, s // tq),
        in_specs=[pl.BlockSpec((1, hg, tq, HEAD_PAD), lambda i, h, j: (i, h, j, 0)),
                  kv_spec, kv_spec],
        out_specs=pl.BlockSpec((1, hg // 2, tq, 2 * V_HEAD), lambda i, h, j: (i, h, j, 0)),
        scratch_shapes=[pltpu.VMEM((2, 2, tq, s), F32), pltpu.VMEM((2, 2, tq, LANES), F32)],
        compiler_params=_cparams(3),
        name="attn",
    )(q, k, v)


def _pair_expand(cols, low_half):
    return jnp.concatenate([jnp.where(low_half, cols[0], cols[1]),
                            jnp.where(low_half, cols[2], cols[3])], axis=1)


def _ssd_kernel(z_ref, xbc_ref, dtt_ref, cw_ref, cbias_ref, dtb_ref, alog_ref, dsk_ref, gn_ref,
                o_ref, xp_ref, bt_ref, cs_ref, cb_ref, acc_ref, col_ref, row_ref, h_ref):
    s = xbc_ref.shape[1]
    q = SSD_CHUNK
    n = SSD_STATE
    nc = s // q
    nh2 = 2 * HPG

    a2 = -jnp.exp(alog_ref[0]) * LOG2E
    dt = jax.nn.softplus(dtt_ref[0] + dtb_ref[0][None]).reshape(nc * nh2, q)
    pre = suf = (dt.reshape(nc, nh2, q) * a2[None]).reshape(nc * nh2, q)
    lane = lax.broadcasted_iota(jnp.int32, (nc * nh2, q), 1)
    is_fwd = (lax.broadcasted_iota(jnp.int32, (nc * nh2, q), 0) % nh2) < HPG
    k = 1
    while k < q:
        pre = pre + jnp.where(lane >= k, pltpu.roll(pre, k, axis=1), 0.0)
        suf = suf + jnp.where(lane < q - k, pltpu.roll(suf, q - k, axis=1), 0.0)
        k *= 2
    cum = jnp.where(is_fwd, pre, suf)
    end = jnp.where(is_fwd, jnp.broadcast_to(cum[:, q - 1:q], cum.shape),
                    jnp.broadcast_to(cum[:, 0:1], cum.shape))
    row_ref[:, 0:nh2, :] = (cum - jnp.log2(dt)).reshape(nc, nh2, q)
    row_ref[:, nh2:2 * nh2, :] = (dt * jnp.exp2(end - cum)).reshape(nc, nh2, q)
    row_ref[:, 2 * nh2:3 * nh2, :] = jnp.exp2(end).reshape(nc, nh2, q)
    row_ref[:, 3 * nh2:4 * nh2, :] = cum.reshape(nc, nh2, q)
    zeros_pad = jnp.zeros((q - nh2, q), F32)

    w = cw_ref[...]
    bias = cbias_ref[...]
    dsk = dsk_ref[0]
    pad = CONV_WIDTH // 2

    win = 2 * q
    sh_r = lax.broadcasted_iota(jnp.int32, (q, win), 0)
    sh_c = lax.broadcasted_iota(jnp.int32, (q, win), 1)
    taps = [t for t in range(CONV_WIDTH) if t != pad]
    shifts = jnp.concatenate([jnp.where(sh_c == sh_r + (HALO - pad + t), 1.0, 0.0).astype(BF16)
                              for t in taps], axis=0)

    zero_halo = jnp.zeros((HALO, GW + 2 * n), BF16)
    tail = jnp.zeros((win - q - 2 * HALO, GW + 2 * n), BF16)

    col_halves = (slice(0, GW), slice(GW, 2 * GW))
    lane_head = lax.broadcasted_iota(jnp.int32, (q, GW), 1) // SSD_HEAD_DIM

    def conv_shift(c):
        r0 = pl.multiple_of(c * q, q)
        main = xbc_ref[0, pl.ds(r0, q), :]
        p0 = pl.multiple_of(jnp.maximum(r0 - HALO, 0), HALO)
        n0 = pl.multiple_of(jnp.minimum(r0 + q, s - HALO), HALO)
        prev = jnp.where(c > 0, xbc_ref[0, pl.ds(p0, HALO), :], zero_halo)
        nxt = jnp.where(c < nc - 1, xbc_ref[0, pl.ds(n0, HALO), :], zero_halo)
        window = jnp.concatenate([prev, main, nxt, tail], axis=0)
        return main, [jnp.dot(shifts, window[:, sl], preferred_element_type=F32)
                      for sl in col_halves]

    def conv_finish(c, main, shifted):
        halves = []
        for sl, sh in zip(col_halves, shifted):
            out = bias[:, sl] + main[:, sl].astype(F32) * w[pad:pad + 1, sl]
            for i, t in enumerate(taps):
                out = out + sh[i * q:(i + 1) * q] * w[t:t + 1, sl]
            halves.append(_silu(out))
        xc = halves[0]
        bconv = halves[1][:, 0:n]
        cmat = halves[1][:, n:2 * n].astype(BF16)
        xb = xc.astype(BF16)
        zero_b = jnp.zeros_like(xb)
        for jp in range(HPG // 2):
            xp_ref[c, jp] = jnp.concatenate(
                [jnp.where(lane_head == j, xb, zero_b) for j in (2 * jp, 2 * jp + 1)], axis=0)
        acc_ref[c] = xc * dsk
        cs_ref[c] = cmat
        col_ref[c] = jnp.concatenate([row_ref[c, 3 * nh2:4 * nh2, :], zeros_pad], axis=0).T
        bt_ref[c] = bconv.T
        cb_ref[c] = lax.dot_general(cmat, bconv.astype(BF16), (((1,), (1,)), ((), ())),
                                    preferred_element_type=F32)

    conv_group = 8 if nc % 8 == 0 else 2

    def conv_body(i, carry):
        cs = [i * conv_group + u for u in range(conv_group)]
        staged = [conv_shift(c) for c in cs]
        for c, (main, shifted) in zip(cs, staged):
            conv_finish(c, main, shifted)
        return carry

    lax.fori_loop(0, nc // conv_group, conv_body, 0)

    h_ref[...] = jnp.zeros_like(h_ref)
    row_i = lax.broadcasted_iota(jnp.int32, (q, q), 0)
    col_i = lax.broadcasted_iota(jnp.int32, (q, q), 1)
    low_half = col_i < SSD_HEAD_DIM
    low_row = low_half[0:1, :]
    masks = (row_i >= col_i, row_i <= col_i)

    def scan_local(c, d):
        colblk = col_ref[c]
        cb = cb_ref[c]
        bt = bt_ref[c]
        row = lambda kind, j: row_ref[c, kind * nh2 + d * HPG + j:kind * nh2 + d * HPG + j + 1, :]
        cum_b = [jnp.broadcast_to(colblk[:, d * HPG + j:d * HPG + j + 1], (q, LANES))
                 for j in range(HPG)]
        ydiag = upd = None
        for jp in range(HPG // 2):
            ms, bts = [], []
            for j in (2 * jp, 2 * jp + 1):
                dec = jnp.exp2(jnp.where(masks[d], cum_b[j] - row(0, j), -jnp.inf))
                ms.append((cb * dec).astype(BF16))
                bts.append((bt * row(1, j)).astype(BF16))
            lhs = jnp.concatenate([jnp.concatenate(ms, axis=1), jnp.concatenate(bts, axis=1)],
                                  axis=0)
            both = jnp.dot(lhs, xp_ref[c, jp], preferred_element_type=F32)
            ydiag = both[0:q] if ydiag is None else ydiag + both[0:q]
            upd = both[q:q + n] if upd is None else upd + both[q:q + n]
        in_scale = jnp.exp2(_pair_expand(cum_b, low_half))
        decay = _pair_expand([row(2, j) for j in range(HPG)], low_row)
        return ydiag, upd, in_scale, decay

    def direction(c, d):
        ydiag, upd, in_scale, decay = scan_local(c, d)
        hprev = h_ref[d]
        yoff = jnp.dot(cs_ref[c], hprev.astype(BF16), preferred_element_type=F32)
        acc_ref[c] += ydiag + yoff * in_scale
        h_ref[d] = hprev * decay + upd

    def scan_body(i, carry):
        direction(i, 0)
        direction(nc - 1 - i, 1)
        return carry

    lax.fori_loop(0, nc, scan_body, 0, unroll=16 if nc % 16 == 0 else 2)

    gn = gn_ref[...]

    def out_body(c, carry):
        r0 = pl.multiple_of(c * q, q)
        y = acc_ref[c] * _silu(z_ref[0, pl.ds(r0, q), :].astype(F32))
        o_ref[0, pl.ds(r0, q), :] = (_rms(y) * gn).astype(BF16)
        return carry

    lax.fori_loop(0, nc, out_body, 0, unroll=8 if nc % 8 == 0 else 2)


def _ssd(main, dtt, cw, cb, dtb, alog, dsk, gn):
    b, s, _ = main.shape
    g = SSD_GROUPS
    n = SSD_STATE
    q = SSD_CHUNK
    nc = s // q
    gc = GW + 2 * n
    xbc_blk0 = (g * GW) // gc
    return pl.pallas_call(
        _ssd_kernel,
        out_shape=jax.ShapeDtypeStruct((b, s, g * GW), BF16),
        grid=(b, g),
        in_specs=[pl.BlockSpec((1, s, GW), lambda i, j: (i, 0, j)),
                  pl.BlockSpec((1, s, gc), lambda i, j: (i, 0, xbc_blk0 + j)),
                  pl.BlockSpec((1, nc, 2 * HPG, q), lambda i, j: (i, 0, j, 0)),
                  pl.BlockSpec((CONV_WIDTH, gc), lambda i, j: (0, j)),
                  pl.BlockSpec((1, gc), lambda i, j: (0, j)),
                  pl.BlockSpec((1, 2 * HPG, LANES), lambda i, j: (j, 0, 0)),
                  pl.BlockSpec((1, 2 * HPG, LANES), lambda i, j: (j, 0, 0)),
                  pl.BlockSpec((1, 1, GW), lambda i, j: (j, 0, 0)),
                  pl.BlockSpec((1, GW), lambda i, j: (0, j))],
        out_specs=pl.BlockSpec((1, s, GW), lambda i, j: (i, 0, j)),
        scratch_shapes=[pltpu.VMEM((nc, HPG // 2, 2 * q, GW), BF16),
                        pltpu.VMEM((nc, n, q), F32),
                        pltpu.VMEM((nc, q, n), BF16),
                        pltpu.VMEM((nc, q, q), F32),
                        pltpu.VMEM((nc, q, GW), F32),
                        pltpu.VMEM((nc, q, LANES), F32),
                        pltpu.VMEM((nc, 8 * HPG, q), F32),
                        pltpu.VMEM((2, n, GW), F32)],
        compiler_params=_cparams(2),
        name="ssd",
    )(main, main, dtt, cw, cb, dtb, alog, dsk, gn)


def _merge_kernel(yg_ref, at_ref, gt_ref, x_ref, g1_ref, wa_ref, wb_ref, wo_ref, o_ref):
    d = x_ref.shape[2]
    ya = jnp.dot(yg_ref[0], wa_ref[...], preferred_element_type=F32)
    attn = jnp.concatenate([at_ref[0, p] for p in range(at_ref.shape[1])], axis=1)
    yb = jnp.dot(attn, wb_ref[...], preferred_element_type=F32)
    gates = jax.nn.sigmoid(gt_ref[0].astype(F32))
    mixed = (gates[:, 0:d] * ya + gates[:, d:2 * d] * yb).astype(BF16)
    o_ref[0] = x_ref[0] + g1_ref[0] * jnp.dot(mixed, wo_ref[...], preferred_element_type=F32)


def _merge(yg, attn, main, x, gate1, wa, wb, wo, tm):
    b, s, d = x.shape
    gate_blk = (main.shape[2] - 2 * d) // (2 * d)
    return pl.pallas_call(
        _merge_kernel,
        out_shape=jax.ShapeDtypeStruct((b, s, d), F32),
        grid=(b, s // tm),
        in_specs=[pl.BlockSpec((1, tm, yg.shape[2]), lambda i, j: (i, j, 0)),
                  pl.BlockSpec((1, attn.shape[1], tm, attn.shape[3]), lambda i, j: (i, 0, j, 0)),
                  pl.BlockSpec((1, tm, 2 * d), lambda i, j: (i, j, gate_blk)),
                  pl.BlockSpec((1, tm, d), lambda i, j: (i, j, 0)),
                  pl.BlockSpec((1, 1, d), lambda i, j: (i, 0, 0)),
                  _resident(wa.shape), _resident(wb.shape), _resident(wo.shape)],
        out_specs=pl.BlockSpec((1, tm, d), lambda i, j: (i, j, 0)),
        compiler_params=_cparams(2),
        name="merge",
    )(yg, attn, main, x, gate1, wa, wb, wo)


def _mlp_kernel(x_ref, sc_ref, sh_ref, g2_ref, gn_ref, gf_ref, w1_ref, w2_ref, o_ref,
                *, ff_chunk):
    x = x_ref[0]
    h = _rms(x) * gn_ref[...]
    hb = (h * (1.0 + sc_ref[0]) + sh_ref[0]).astype(BF16)
    acc = None
    for j in range(w1_ref.shape[1] // ff_chunk):
        sl = slice(j * ff_chunk, (j + 1) * ff_chunk)
        u = jnp.maximum(jnp.dot(hb, w1_ref[:, sl], preferred_element_type=F32), 0.0)
        term = jnp.dot((u * u).astype(BF16), w2_ref[sl, :], preferred_element_type=F32)
        acc = term if acc is None else acc + term
    y = x + g2_ref[0] * acc
    o_ref[0] = _rms(y) * gf_ref[...]


def _mlp(x, scale, shift, gate2, gn, gf, w1, w2, tm):
    b, s, d = x.shape
    vec = pl.BlockSpec((1, 1, d), lambda i, j: (i, 0, 0))
    return pl.pallas_call(
        functools.partial(_mlp_kernel, ff_chunk=1024),
        out_shape=jax.ShapeDtypeStruct((b, s, d), F32),
        grid=(b, s // tm),
        in_specs=[pl.BlockSpec((1, tm, d), lambda i, j: (i, j, 0)), vec, vec, vec,
                  _resident(gn.shape), _resident(gf.shape),
                  _resident(w1.shape), _resident(w2.shape)],
        out_specs=pl.BlockSpec((1, tm, d), lambda i, j: (i, j, 0)),
        compiler_params=_cparams(2),
        name="mlp",
    )(x, scale, shift, gate2, gn, gf, w1, w2)


def _rope_tables(s):
    inv = 1.0 / (ROPE_THETA ** (jnp.arange(0, QK_ROPE, 2, dtype=F32) / QK_ROPE))
    ang = jnp.arange(s, dtype=F32)[:, None] * inv[None, :]
    cos, sin = jnp.cos(ang), jnp.sin(ang)
    zn = jnp.zeros((s, QK_NOPE), F32)
    zp = jnp.zeros((s, HEAD_PAD - QK_NOPE - QK_ROPE), F32)
    kcos = jnp.concatenate([zn, cos, cos, zp], axis=1)
    ksin = jnp.concatenate([zn, -sin, sin, zp], axis=1)
    scale = (QK_NOPE + QK_ROPE) ** -0.5 * LOG2E
    qcos = jnp.concatenate([jnp.ones((s, QK_NOPE), F32), cos, cos, zp], axis=1) * scale
    qsin = ksin * scale
    return qcos, qsin, kcos, ksin


def _group_xbc(a, d_inner):
    lead = a.shape[:-1]
    nbc = SSD_GROUPS * SSD_STATE
    x = a[..., :d_inner].reshape(lead + (SSD_GROUPS, GW))
    bm = a[..., d_inner:d_inner + nbc].reshape(lead + (SSD_GROUPS, SSD_STATE))
    cm = a[..., d_inner + nbc:].reshape(lead + (SSD_GROUPS, SSD_STATE))
    return jnp.concatenate([x, bm, cm], axis=-1).reshape(lead + (-1,))


def _prep_weights(w_in, w_q_b, w_kv_b, d_inner):
    d = w_in.shape[0]
    conv_ch = d_inner + 2 * SSD_GROUPS * SSD_STATE
    n_heads = d_inner // SSD_HEAD_DIM
    o_z, o_x = 0, d_inner
    o_dt = o_x + conv_ch
    o_q = o_dt + 2 * n_heads
    o_kv = o_q + Q_LORA
    o_g = o_kv + KV_LORA + QK_ROPE
    half = QK_ROPE // 2
    w_main = jnp.concatenate([w_in[:, o_z:o_x], _group_xbc(w_in[:, o_x:o_dt], d_inner),
                              w_in[:, o_g:]], axis=1).astype(BF16)
    kr = w_in[:, o_kv + KV_LORA:o_g]
    kr_sw = jnp.concatenate([kr[:, half:], kr[:, :half]], axis=1)
    zl = jnp.zeros((d, QK_NOPE), F32)
    zr = jnp.zeros((d, HEAD_PAD - QK_NOPE - QK_ROPE), F32)
    w_small = jnp.concatenate(
        [w_in[:, o_q:o_kv], w_in[:, o_kv:o_kv + KV_LORA], zl, kr, zr, zl, kr_sw, zr],
        axis=1).astype(BF16)
    w_dt = w_in[:, o_dt:o_q].T.reshape(2, SSD_GROUPS, HPG, d)
    w_dt = jnp.transpose(w_dt, (1, 0, 2, 3)).reshape(2 * n_heads, d).astype(BF16)

    lq = w_q_b.shape[0]
    wq3 = w_q_b.reshape(lq, MLA_HEADS, QK_NOPE + QK_ROPE)
    q_nope, q_r = wq3[..., :QK_NOPE], wq3[..., QK_NOPE:]
    zq = jnp.zeros((lq, MLA_HEADS, HEAD_PAD - QK_NOPE - QK_ROPE), F32)
    wq = jnp.concatenate([q_nope, q_r, zq], axis=-1).reshape(lq, -1).astype(BF16)
    lk = w_kv_b.shape[0]
    wkv3 = w_kv_b.reshape(lk, MLA_HEADS, QK_NOPE + V_HEAD)
    zk = jnp.zeros((lk, MLA_HEADS, HEAD_PAD - QK_NOPE), F32)
    wk = jnp.concatenate([wkv3[..., :QK_NOPE], zk], axis=-1).reshape(lk, -1).astype(BF16)
    v4 = wkv3[..., QK_NOPE:].reshape(lk, MLA_HEADS // 2, 2, V_HEAD)
    zv = jnp.zeros((lk, MLA_HEADS // 2, V_HEAD), F32)
    wv = jnp.stack([jnp.concatenate([v4[:, :, 0], zv], axis=-1),
                    jnp.concatenate([zv, v4[:, :, 1]], axis=-1)], axis=2)
    wv = wv.reshape(lk, MLA_HEADS * HEAD_PAD).astype(BF16)
    pair_ones = jnp.zeros((2 * HEAD_PAD,), F32).at[V_HEAD].set(1.0).at[HEAD_PAD].set(1.0)
    vone = jnp.tile(pair_ones, MLA_HEADS // 2).reshape(1, MLA_HEADS * HEAD_PAD)
    return w_main, w_small, w_dt, wq, wk, wv, vone


def _group_rows(v):
    f, bwd = v
    g = jnp.concatenate([f.reshape(SSD_GROUPS, HPG), bwd.reshape(SSD_GROUPS, HPG)], axis=1)
    return jnp.broadcast_to(g[:, :, None], (SSD_GROUPS, 2 * HPG, LANES)).astype(F32)


def kernel(x_prompt, x_sample, c_prompt, c_sample, w_ada, b_ada, g_norm1, w_in, conv_w,
           conv_b, dt_bias_fwd, dt_bias_bwd, a_log_fwd, a_log_bwd, d_skip, g_ssd_norm,
           w_ssd_out, g_q_norm, w_q_b, g_kv_norm, w_kv_b, w_mla_out, w_o, g_norm2,
           w_mlp_in, w_mlp_out, g_final):
    assert w_ada.shape[0] == 1, "single layer"
    d = x_prompt.shape[2]
    d_inner = w_ssd_out.shape[1]
    w_main, w_small, w_dt, wq, wk, wv, vone = _prep_weights(
        w_in[0], w_q_b[0], w_kv_b[0], d_inner)
    wa, wb, wo = (w_ssd_out[0].astype(BF16), w_mla_out[0].astype(BF16), w_o[0].astype(BF16))
    w1, w2 = w_mlp_in[0].astype(BF16), w_mlp_out[0].astype(BF16)
    row = lambda v: v.reshape(1, -1).astype(F32)
    dtb = _group_rows((dt_bias_fwd[0], dt_bias_bwd[0]))
    alog = _group_rows((a_log_fwd[0], a_log_bwd[0]))
    dsk = jnp.repeat(d_skip[0].astype(F32), SSD_HEAD_DIM).reshape(SSD_GROUPS, 1, GW)
    cw = _group_xbc(conv_w[0].astype(F32), d_inner)
    cbias = _group_xbc(row(conv_b[0]), d_inner)

    nb = c_prompt.shape[0]
    ada = _ada(jnp.concatenate([c_prompt, c_sample], axis=0), w_ada[0], b_ada[0])

    def trunk(x, ada_rows):
        b, s, _ = x.shape
        mods = [ada_rows[:, None, i * d:(i + 1) * d] for i in range(N_ADA)]
        shift1, scale1, gate1, shift2, scale2, gate2 = mods
        tm = min(512, s)
        main, small, dtt = _inproj(x, scale1, shift1, row(g_norm1[0]), w_main, w_small, w_dt, tm)
        q, k, v = _qkv(small, _rope_tables(s), row(g_q_norm[0]), wq,
                       row(g_kv_norm[0]), wk, wv, vone, tm)
        tq = min(s, max(SSD_CHUNK, ATTN_LOGIT_ELEMS // s))
        attn = _attn(q, k, v, tq, min(512, s))
        yg = _ssd(main, dtt, cw, cbias, dtb, alog, dsk, row(g_ssd_norm[0]))
        x1 = _merge(yg, attn, main, x, gate1, wa, wb, wo, tm)
        return _mlp(x1, scale2, shift2, gate2, row(g_norm2[0]), row(g_final), w1, w2, tm)

    return trunk(x_prompt, ada[:nb]), trunk(x_sample, ada[nb:])
```
